```python
import jax, jax.numpy as jnp
from jax import lax
import numpy as np

D_MODEL = 1024
BATCH = 8
SEQ = 4096
DEPTH = 2

HEAD_DIM = 64
N_MIX_HEADS = 12
N_MEM_HEADS = 4
MEM_LEN = 256
DILATED_GROUPS = ((128, 1), (512, 4), (2048, 16))
HEADS_PER_GROUP = N_MIX_HEADS // len(DILATED_GROUPS)
ROT_DIM = HEAD_DIM // 4
ROT_HALF = ROT_DIM // 2
ROPE_THETA = 500000.0
D_FF = 2816
BLOCK = 128
N_MIXERS = 2
N_A_LAYERS = (DEPTH + 1) // 2
N_B_LAYERS = DEPTH // 2
DEEPNORM_ALPHA = (2 * DEPTH) ** 0.25
DEEPNORM_BETA = (8 * DEPTH) ** -0.25
LN_EPS = 1e-5
MIX_W = N_MIX_HEADS * HEAD_DIM
MEM_W = N_MEM_HEADS * HEAD_DIM
A_IN_W = 3 * MIX_W + MEM_W
B_IN_W = 3 * MIX_W + N_MIX_HEADS + MEM_W
A_OUT_IN = HEADS_PER_GROUP * HEAD_DIM + MEM_W
B_OUT_IN = MIX_W + MEM_W
ATTN_SCALE = HEAD_DIM ** -0.5

kernel_name = "hybrid_dilated_fox_macaron_deepnorm"


def layer_norm(x, g, b):
    xf = x.astype(jnp.float32)
    mu = jnp.mean(xf, axis=-1, keepdims=True)
    var = jnp.mean(jnp.square(xf - mu), axis=-1, keepdims=True)
    y = (xf - mu) * lax.rsqrt(var + LN_EPS) * g.astype(jnp.float32) + b.astype(jnp.float32)
    return y.astype(x.dtype)


def swiglu(x, w_gate_up, w_down):
    gate, up = jnp.split(x @ w_gate_up, 2, axis=-1)
    return (jax.nn.silu(gate) * up) @ w_down


def rope_partial(t, cos, sin):
    c = cos[None, :, None, :].astype(t.dtype)
    s = sin[None, :, None, :].astype(t.dtype)
    t1 = t[..., :ROT_HALF]
    t2 = t[..., ROT_HALF:ROT_DIM]
    return jnp.concatenate([t1 * c - t2 * s, t2 * c + t1 * s, t[..., ROT_DIM:]], axis=-1)


def banded_causal_attention(q, k, v, n_back):
    N, L, H, dh = q.shape
    nb = -(-L // BLOCK)
    pad = nb * BLOCK - L
    padw = ((0, 0), (0, pad), (0, 0), (0, 0))
    q = jnp.pad(q, padw)
    k = jnp.pad(k, padw)
    v = jnp.pad(v, padw)
    qb = q.reshape(N, nb, BLOCK, H, dh)

    def with_prev(t):
        tb = t.reshape(N, nb, BLOCK, H, dh)
        prev = jnp.concatenate([jnp.zeros_like(tb[:, :1]), tb[:, :-1]], axis=1)
        return jnp.concatenate([prev, tb], axis=2)

    kk = with_prev(k)
    vv = with_prev(v)
    s = jnp.einsum('nbqhd,nbkhd->nbhqk', qb, kk).astype(jnp.float32) * ATTN_SCALE
    qi = jnp.arange(BLOCK)[:, None]
    ki = jnp.arange(2 * BLOCK)[None, :]
    dist = qi + BLOCK - ki
    kpos = jnp.arange(nb)[:, None, None] * BLOCK - BLOCK + ki[None]
    mask = (dist >= 0) & (dist <= n_back) & (kpos >= 0)
    s = jnp.where(mask[None, :, None], s, -jnp.inf)
    lse = jax.nn.logsumexp(s, axis=-1)
    p = jnp.exp(s - lse[..., None])
    o = jnp.einsum('nbhqk,nbkhd->nbqhd', p.astype(v.dtype), vv)
    o = o.reshape(N, nb * BLOCK, H, dh)[:, :L]
    lse = lse.transpose(0, 1, 3, 2).reshape(N, nb * BLOCK, H)[:, :L]
    return o, lse


def dilated_group_attention(q, k, v, window, dilation):
    B, S, H, dh = q.shape
    r = dilation
    L = S // r

    def split(t):
        return t.reshape(B, L, r, H, dh).transpose(0, 2, 1, 3, 4).reshape(B * r, L, H, dh)

    o, lse = banded_causal_attention(split(q), split(k), split(v), window // r)
    o = o.reshape(B, r, L, H, dh).transpose(0, 2, 1, 3, 4).reshape(B, S, H, dh)
    lse = lse.reshape(B, r, L, H).transpose(0, 2, 1, 3).reshape(B, S, H)
    return o, lse


def memory_attention(q_mem, mem, w_kv):
    B, S, _ = q_mem.shape
    M = mem.shape[1]
    qm = q_mem.reshape(B, S, N_MEM_HEADS, HEAD_DIM)
    km, vm = jnp.split(mem @ w_kv, 2, axis=-1)
    km = km.reshape(B, M, N_MEM_HEADS, HEAD_DIM)
    vm = vm.reshape(B, M, N_MEM_HEADS, HEAD_DIM)
    s = jnp.einsum('bshd,bmhd->bhsm', qm, km).astype(jnp.float32) * ATTN_SCALE
    p = jax.nn.softmax(s, axis=-1)
    return jnp.einsum('bhsm,bmhd->bshd', p.astype(vm.dtype), vm)


def dilated_mixer(x, mem, w_in, w_mem_kv, w_out, cos, sin):
    B, S, _ = x.shape
    h = x @ w_in
    q = h[..., :MIX_W].reshape(B, S, N_MIX_HEADS, HEAD_DIM)
    k = h[..., MIX_W:2 * MIX_W].reshape(B, S, N_MIX_HEADS, HEAD_DIM)
    v = h[..., 2 * MIX_W:3 * MIX_W].reshape(B, S, N_MIX_HEADS, HEAD_DIM)
    q_mem = h[..., 3 * MIX_W:]
    q = rope_partial(q, cos, sin)
    k = rope_partial(k, cos, sin)
    outs, lses = [], []
    for g, (window, dilation) in enumerate(DILATED_GROUPS):
        sl = slice(g * HEADS_PER_GROUP, (g + 1) * HEADS_PER_GROUP)
        o, l = dilated_group_attention(q[:, :, sl], k[:, :, sl], v[:, :, sl], window, dilation)
        outs.append(o)
        lses.append(l)
    alpha = jax.nn.softmax(jnp.stack(lses, axis=0), axis=0)
    o_a = jnp.einsum('gbsh,gbshd->bshd', alpha.astype(x.dtype), jnp.stack(outs, axis=0))
    o_m = memory_attention(q_mem, mem, w_mem_kv)
    cat = jnp.concatenate([o_a.reshape(B, S, -1), o_m.reshape(B, S, -1)], axis=-1)
    return cat @ w_out


def fox_attention(q, k, v, logf):
    B, S, H, dh = q.shape
    nq = S // BLOCK
    c = jnp.cumsum(logf, axis=1).transpose(0, 2, 1)
    qb = q.reshape(B, nq, BLOCK, H, dh).transpose(1, 0, 2, 3, 4)
    cqb = c.reshape(B, H, nq, BLOCK).transpose(2, 0, 1, 3)
    kpos = jnp.arange(S)

    def block(args):
        i, q_i, cq_i = args
        s = jnp.einsum('bqhd,bkhd->bhqk', q_i, k).astype(jnp.float32) * ATTN_SCALE
        s = s + cq_i[..., None] - c[:, :, None, :]
        qpos = i * BLOCK + jnp.arange(BLOCK)
        s = jnp.where(kpos[None, :] <= qpos[:, None], s, -jnp.inf)
        p = jax.nn.softmax(s, axis=-1)
        return jnp.einsum('bhqk,bkhd->bqhd', p.astype(v.dtype), v)

    o = lax.map(block, (jnp.arange(nq), qb, cqb))
    return o.transpose(1, 0, 2, 3, 4).reshape(B, S, H, dh)


def forgetting_mixer(x, mem, w_in, forget_bias, w_mem_kv, w_out):
    B, S, _ = x.shape
    h = x @ w_in
    q = h[..., :MIX_W].reshape(B, S, N_MIX_HEADS, HEAD_DIM)
    k = h[..., MIX_W:2 * MIX_W].reshape(B, S, N_MIX_HEADS, HEAD_DIM)
    v = h[..., 2 * MIX_W:3 * MIX_W].reshape(B, S, N_MIX_HEADS, HEAD_DIM)
    f_logit = h[..., 3 * MIX_W:3 * MIX_W + N_MIX_HEADS].astype(jnp.float32)
    q_mem = h[..., 3 * MIX_W + N_MIX_HEADS:]
    logf = jax.nn.log_sigmoid(f_logit + forget_bias.astype(jnp.float32))
    o_b = fox_attention(q, k, v, logf)
    o_m = memory_attention(q_mem, mem, w_mem_kv)
    cat = jnp.concatenate([o_b.reshape(B, S, -1), o_m.reshape(B, S, -1)], axis=-1)
    return cat @ w_out


def setup_inputs(seed: int = 0) -> dict:
    key = jax.random.key(seed)
    ks = jax.random.split(key, 16)

    def nrm(k, shape, fan_in):
        return jax.random.normal(k, shape, jnp.float32) * fan_in ** -0.5

    x = jax.random.normal(ks[0], (BATCH, SEQ, D_MODEL), jnp.float32)
    mem = jax.random.normal(ks[1], (BATCH, MEM_LEN, D_MODEL), jnp.float32)
    ffn1_w_gate_up = nrm(ks[2], (DEPTH, D_MODEL, 2 * D_FF), D_MODEL)
    ffn1_w_down = nrm(ks[3], (DEPTH, D_FF, D_MODEL), D_FF) * DEEPNORM_BETA
    ffn2_w_gate_up = nrm(ks[4], (DEPTH, D_MODEL, 2 * D_FF), D_MODEL)
    ffn2_w_down = nrm(ks[5], (DEPTH, D_FF, D_MODEL), D_FF) * DEEPNORM_BETA
    ln_gain = 1.0 + 0.02 * jax.random.normal(ks[6], (DEPTH, 3, D_MODEL), jnp.float32)
    ln_bias = 0.02 * jax.random.normal(ks[7], (DEPTH, 3, D_MODEL), jnp.float32)
    mem_w_kv = nrm(ks[8], (DEPTH, D_MODEL, 2 * MEM_W), D_MODEL)
    a_w_in = nrm(ks[9], (N_A_LAYERS, D_MODEL, A_IN_W), D_MODEL)
    a_w_out = nrm(ks[10], (N_A_LAYERS, A_OUT_IN, D_MODEL), A_OUT_IN) * DEEPNORM_BETA
    b_w_in = nrm(ks[11], (N_B_LAYERS, D_MODEL, B_IN_W), D_MODEL)
    b_forget_bias = jax.random.uniform(ks[12], (N_B_LAYERS, N_MIX_HEADS), jnp.float32, 1.0, 4.0)
    b_w_out = nrm(ks[13], (N_B_LAYERS, B_OUT_IN, D_MODEL), B_OUT_IN) * DEEPNORM_BETA
    return {"x": x, "mem": mem,
            "ffn1_w_gate_up": ffn1_w_gate_up, "ffn1_w_down": ffn1_w_down,
            "ffn2_w_gate_up": ffn2_w_gate_up, "ffn2_w_down": ffn2_w_down,
            "ln_gain": ln_gain, "ln_bias": ln_bias, "mem_w_kv": mem_w_kv,
            "a_w_in": a_w_in, "a_w_out": a_w_out,
            "b_w_in": b_w_in, "b_forget_bias": b_forget_bias, "b_w_out": b_w_out}


def reference(x, mem, ffn1_w_gate_up, ffn1_w_down, ffn2_w_gate_up, ffn2_w_down,
              ln_gain, ln_bias, mem_w_kv, a_w_in, a_w_out, b_w_in, b_forget_bias, b_w_out):
    pos = jnp.arange(x.shape[1], dtype=jnp.float32)
    inv_freq = 1.0 / (ROPE_THETA ** (jnp.arange(ROT_HALF, dtype=jnp.float32) / ROT_HALF))
    ang = pos[:, None] * inv_freq[None, :]
    cos = jnp.cos(ang)
    sin = jnp.sin(ang)
    for i in range(DEPTH):
        j = i // N_MIXERS
        x = layer_norm(DEEPNORM_ALPHA * x + 0.5 * swiglu(x, ffn1_w_gate_up[i], ffn1_w_down[i]),
                       ln_gain[i, 0], ln_bias[i, 0])
        if i % N_MIXERS == 0:
            mix = dilated_mixer(x, mem, a_w_in[j], mem_w_kv[i], a_w_out[j], cos, sin)
        else:
            mix = forgetting_mixer(x, mem, b_w_in[j], b_forget_bias[j], mem_w_kv[i], b_w_out[j])
        x = layer_norm(DEEPNORM_ALPHA * x + mix, ln_gain[i, 1], ln_bias[i, 1])
        x = layer_norm(DEEPNORM_ALPHA * x + 0.5 * swiglu(x, ffn2_w_gate_up[i], ffn2_w_down[i]),
                       ln_gain[i, 2], ln_bias[i, 2])
    return x
```

```python
import functools

import jax
import jax.numpy as jnp
from jax import lax
from jax.experimental import pallas as pl
from jax.experimental.pallas import tpu as pltpu

D_MODEL = 1024
DEPTH = 2
HEAD_DIM = 64
N_MIX_HEADS = 12
N_MEM_HEADS = 4
DILATED_GROUPS = ((128, 1), (512, 4), (2048, 16))
HEADS_PER_GROUP = N_MIX_HEADS // len(DILATED_GROUPS)
ROT_DIM = HEAD_DIM // 4
ROT_HALF = ROT_DIM // 2
ROPE_THETA = 500000.0
D_FF = 2816
BAND = 128
DEEPNORM_ALPHA = (2 * DEPTH) ** 0.25
LN_EPS = 1e-5
MIX_W = N_MIX_HEADS * HEAD_DIM
MEM_W = N_MEM_HEADS * HEAD_DIM
GROUP_W = HEADS_PER_GROUP * HEAD_DIM
ATTN_SCALE = HEAD_DIM ** -0.5

LANES = 128
VMEM_LIMIT = 56 * 1024 * 1024

ROW_TILE = 512
FF_CHUNK = 256
FOX_TILE = 256
FOX_AUG = 16
DIL_TILE = 256

F32 = jnp.float32
BF16 = jnp.bfloat16
NEG = -1e30


def _cparams(*sem):
    return pltpu.CompilerParams(dimension_semantics=sem, vmem_limit_bytes=VMEM_LIMIT)


def _resident(shape):
    return pl.BlockSpec(shape, lambda *_: (0,) * len(shape), pipeline_mode=pl.Buffered(1))


def _dot(a, b):
    return jnp.dot(a, b, preferred_element_type=F32)


def _dot_nt(a, b):
    return lax.dot_general(a, b, (((1,), (1,)), ((), ())), preferred_element_type=F32)


def _layer_norm(z, g, b):
    mu = jnp.mean(z, axis=-1, keepdims=True)
    zc = z - mu
    var = jnp.mean(zc * zc, axis=-1, keepdims=True)
    return zc * lax.rsqrt(var + LN_EPS) * g + b


def _ffn_kernel(x_ref, wgu_ref, wd_ref, g_ref, b_ref, o_ref, h_ref):
    x = x_ref[...]
    xb = x.astype(BF16)
    for c in range(D_FF // FF_CHUNK):
        lo = c * FF_CHUNK
        gate = _dot(xb, wgu_ref[:, lo:lo + FF_CHUNK])
        up = _dot(xb, wgu_ref[:, D_FF + lo:D_FF + lo + FF_CHUNK])
        h_ref[:, lo:lo + FF_CHUNK] = (gate * jax.nn.sigmoid(gate) * up).astype(BF16)
    y = _dot(h_ref[...], wd_ref[...])
    o_ref[...] = _layer_norm(DEEPNORM_ALPHA * x + 0.5 * y, g_ref[...], b_ref[...])


def _ffn(x, wgu, wd, g, b):
    n = x.shape[0]
    row = pl.BlockSpec((ROW_TILE, D_MODEL), lambda i: (i, 0))
    return pl.pallas_call(
        _ffn_kernel,
        grid=(n // ROW_TILE,),
        in_specs=[row, _resident((D_MODEL, 2 * D_FF)), _resident((D_FF, D_MODEL)),
                  _resident((1, D_MODEL)), _resident((1, D_MODEL))],
        out_specs=row,
        out_shape=jax.ShapeDtypeStruct((n, D_MODEL), F32),
        scratch_shapes=[pltpu.VMEM((ROW_TILE, D_FF), BF16)],
        compiler_params=_cparams("parallel"),
        name="ffn_ln",
    )(x, wgu, wd, g, b)


def _memkv_kernel(m_ref, w_ref, o_ref):
    o_ref[...] = _dot(m_ref[...].astype(BF16), w_ref[...]).astype(BF16)


def _memkv(mem2d, w):
    n = mem2d.shape[0]
    tile = min(ROW_TILE, n)
    return pl.pallas_call(
        _memkv_kernel,
        grid=(n // tile,),
        in_specs=[pl.BlockSpec((tile, D_MODEL), lambda i: (i, 0)), _resident((D_MODEL, 2 * MEM_W))],
        out_specs=pl.BlockSpec((tile, 2 * MEM_W), lambda i: (i, 0)),
        out_shape=jax.ShapeDtypeStruct((n, 2 * MEM_W), BF16),
        compiler_params=_cparams("parallel"),
        name="mem_kv",
    )(mem2d, w)


def _memory_heads(qm, kv_ref):
    outs = []
    for h in range(N_MEM_HEADS):
        sl = slice(h * HEAD_DIM, (h + 1) * HEAD_DIM)
        km = kv_ref[0, :, h * HEAD_DIM:(h + 1) * HEAD_DIM]
        vm = kv_ref[0, :, MEM_W + h * HEAD_DIM:MEM_W + (h + 1) * HEAD_DIM]
        s = _dot_nt(qm[:, sl], km)
        p = jnp.exp(s - jnp.max(s, axis=-1, keepdims=True))
        l = jnp.sum(p, axis=-1, keepdims=True)
        outs.append(_dot(p.astype(BF16), vm) / l)
    return jnp.concatenate(outs, axis=-1)


def _proj_a_kernel(x_ref, w_ref, ca_ref, cm_ref, cp_ref, o0_ref, o1_ref, o2_ref, qm_ref):
    xb = x_ref[...].astype(BF16)
    ca, cm, cp = ca_ref[...], cm_ref[...], cp_ref[...]
    qk_w = 2 * GROUP_W
    for g, o_ref in enumerate((o0_ref, o1_ref, o2_ref)):
        h = _dot(xb, w_ref[:, g * 3 * GROUP_W:(g + 1) * 3 * GROUP_W])
        for cb in range(qk_w // LANES):
            t = h[:, cb * LANES:(cb + 1) * LANES]
            r = t * ca + pltpu.roll(t, LANES - ROT_HALF, 1) * cm + pltpu.roll(t, ROT_HALF, 1) * cp
            o_ref[:, cb * LANES:(cb + 1) * LANES] = r.astype(BF16)
        o_ref[:, qk_w:] = h[:, qk_w:].astype(BF16)
    qm_ref[...] = _dot(xb, w_ref[:, 3 * MIX_W:]).astype(BF16)


def _proj_a(x, w, ca, cm, cp, seq):
    n = x.shape[0]
    per_seq = seq // ROW_TILE
    row = lambda w_: pl.BlockSpec((ROW_TILE, w_), lambda i: (i, 0))
    tab = pl.BlockSpec((ROW_TILE, LANES), lambda i: (i % per_seq, 0))
    return pl.pallas_call(
        _proj_a_kernel,
        grid=(n // ROW_TILE,),
        in_specs=[row(D_MODEL), _resident((D_MODEL, 3 * MIX_W + MEM_W)), tab, tab, tab],
        out_specs=[row(3 * GROUP_W)] * 3 + [row(MEM_W)],
        out_shape=[jax.ShapeDtypeStruct((n, 3 * GROUP_W), BF16)] * 3 + [jax.ShapeDtypeStruct((n, MEM_W), BF16)],
        compiler_params=_cparams("parallel"),
        name="proj_a",
    )(x, w, ca, cm, cp)


def _dilated_kernel(cur_ref, prev_ref, o_ref, lse_ref):
    first_key = jnp.where(pl.program_id(2) == 0, BAND, 0)
    qi = lax.broadcasted_iota(jnp.int32, (BAND, 2 * BAND), 0)
    ki = lax.broadcasted_iota(jnp.int32, (BAND, 2 * BAND), 1)
    band = (ki >= qi) & (ki <= qi + BAND)
    for sb in range(DIL_TILE // BAND):
        rows = slice(sb * BAND, (sb + 1) * BAND)
        if sb == 0:
            before = prev_ref[0]
            mask = band & (ki >= first_key)
        else:
            before = cur_ref[0, (sb - 1) * BAND:sb * BAND, :]
            mask = band
        kv = jnp.concatenate([before, cur_ref[0, rows, :]], axis=0)
        outs, lses = [], []
        for h in range(HEADS_PER_GROUP):
            q = cur_ref[0, rows, h * HEAD_DIM:(h + 1) * HEAD_DIM]
            k = kv[:, GROUP_W + h * HEAD_DIM:GROUP_W + (h + 1) * HEAD_DIM]
            v = kv[:, 2 * GROUP_W + h * HEAD_DIM:2 * GROUP_W + (h + 1) * HEAD_DIM]
            s = jnp.where(mask, _dot_nt(q, k), NEG)
            m = jnp.max(s, axis=-1, keepdims=True)
            p = jnp.exp(s - m)
            l = jnp.sum(p, axis=-1, keepdims=True)
            outs.append(_dot(p.astype(BF16), v) / l)
            lses.append(jnp.broadcast_to(m + jnp.log(l), (BAND, HEAD_DIM)))
        o_ref[0, rows, :] = jnp.concatenate(outs, axis=-1)
        lse_ref[0, rows, :] = jnp.concatenate(lses, axis=-1)


def _dilated_group(qkv, batch, seq, dilation):
    length = seq // dilation
    tile = min(DIL_TILE, length)
    assert tile == DIL_TILE and length % tile == 0
    per = tile // BAND
    view = qkv.reshape(batch, length, dilation * 3 * GROUP_W)
    out_block = pl.BlockSpec((1, tile, GROUP_W), lambda b, j, t: (b, t, j))
    o, lse = pl.pallas_call(
        _dilated_kernel,
        grid=(batch, dilation, length // tile),
        in_specs=[pl.BlockSpec((1, tile, 3 * GROUP_W), lambda b, j, t: (b, t, j)),
                  pl.BlockSpec((1, BAND, 3 * GROUP_W), lambda b, j, t: (b, jnp.maximum(t * per - 1, 0), j))],
        out_specs=[out_block, out_block],
        out_shape=[jax.ShapeDtypeStruct((batch, length, dilation * GROUP_W), F32)] * 2,
        compiler_params=_cparams("parallel", "parallel", "parallel"),
        name=f"dilated_attn_d{dilation}",
    )(view, view)
    return o.reshape(batch * seq, GROUP_W), lse.reshape(batch * seq, GROUP_W)


def _out_a_kernel(x_ref, o0_ref, o1_ref, o2_ref, l0_ref, l1_ref, l2_ref, qm_ref, kv_ref, w_ref, g_ref, b_ref,
                  y_ref):
    l0, l1, l2 = l0_ref[...], l1_ref[...], l2_ref[...]
    m = jnp.maximum(jnp.maximum(l0, l1), l2)
    e0, e1, e2 = jnp.exp(l0 - m), jnp.exp(l1 - m), jnp.exp(l2 - m)
    o_a = (e0 * o0_ref[...] + e1 * o1_ref[...] + e2 * o2_ref[...]) / (e0 + e1 + e2)
    o_m = _memory_heads(qm_ref[...], kv_ref)
    mix = _dot(o_a.astype(BF16), w_ref[:GROUP_W, :]) + _dot(o_m.astype(BF16), w_ref[GROUP_W:, :])
    y_ref[...] = _layer_norm(DEEPNORM_ALPHA * x_ref[...] + mix, g_ref[...], b_ref[...])


def _out_a(x, os, lses, qm, kv, w, g, b, seq):
    n = x.shape[0]
    per_seq = seq // ROW_TILE
    row = lambda w_: pl.BlockSpec((ROW_TILE, w_), lambda i: (i, 0))
    return pl.pallas_call(
        _out_a_kernel,
        grid=(n // ROW_TILE,),
        in_specs=[row(D_MODEL)] + [row(GROUP_W)] * 6 + [row(MEM_W),
                  pl.BlockSpec((1, kv.shape[1], 2 * MEM_W), lambda i: (i // per_seq, 0, 0)),
                  _resident((GROUP_W + MEM_W, D_MODEL)), _resident((1, D_MODEL)), _resident((1, D_MODEL))],
        out_specs=row(D_MODEL),
        out_shape=jax.ShapeDtypeStruct((n, D_MODEL), F32),
        compiler_params=_cparams("parallel"),
        name="out_a_ln",
    )(x, *os, *lses, qm, kv, w, g, b)


def _decay_kernel(x_ref, wf_ref, fb_ref, ct_ref, cs_ref, carry_ref):
    @pl.when(pl.program_id(1) == 0)
    def _():
        carry_ref[...] = jnp.zeros_like(carry_ref)

    f = _dot(x_ref[...].astype(BF16), wf_ref[...])
    ft = f.T[:FOX_AUG] + fb_ref[...]
    logf = jnp.minimum(ft, 0.0) - jnp.log1p(jnp.exp(-jnp.abs(ft)))
    lane = lax.broadcasted_iota(jnp.int32, logf.shape, 1)
    c = logf
    shift = 1
    while shift < ROW_TILE:
        c = c + jnp.where(lane >= shift, pltpu.roll(c, shift, 1), 0.0)
        shift *= 2
    c = c + carry_ref[...]
    carry_ref[...] = c[:, ROW_TILE - 1:ROW_TILE]
    hi = c.astype(BF16)
    r1 = c - hi.astype(F32)
    mid = r1.astype(BF16)
    lo = (r1 - mid.astype(F32)).astype(BF16)
    ones = jnp.ones((FOX_AUG, ROW_TILE), BF16)
    zeros = jnp.zeros((LANES - 4 * FOX_AUG, ROW_TILE), BF16)
    stack = jnp.concatenate([hi, mid, lo, ones, zeros], axis=0)
    ct_ref[0] = stack
    cs_ref[0] = stack.astype(F32).T.astype(BF16)


def _decay(x, wf, fb, batch, seq):
    per_seq = seq // ROW_TILE
    return pl.pallas_call(
        _decay_kernel,
        grid=(batch, per_seq),
        in_specs=[pl.BlockSpec((ROW_TILE, D_MODEL), lambda b, i: (b * per_seq + i, 0)),
                  _resident((D_MODEL, LANES)), _resident((FOX_AUG, 1))],
        out_specs=[pl.BlockSpec((1, LANES, ROW_TILE), lambda b, i: (b, 0, i)),
                   pl.BlockSpec((1, ROW_TILE, LANES), lambda b, i: (b, i, 0))],
        out_shape=[jax.ShapeDtypeStruct((batch, LANES, seq), BF16), jax.ShapeDtypeStruct((batch, seq, LANES), BF16)],
        scratch_shapes=[pltpu.VMEM((FOX_AUG, 1), F32)],
        compiler_params=_cparams("parallel", "arbitrary"),
        name="decay_cumsum",
    )(x, wf, fb)


def _proj_b_kernel(x_ref, ct_ref, cs_ref, wqt_ref, wk_ref, wvt_ref, wqm_ref, pq_ref, pk_ref,
                   qt_ref, k_ref, vt_ref, qm_ref):
    xb = x_ref[...].astype(BF16)
    qt = _dot_nt(wqt_ref[...], xb)
    aug = _dot(pq_ref[...], ct_ref[0])
    vt = _dot_nt(wvt_ref[...], xb)
    kk = _dot(xb, wk_ref[...]) + _dot(cs_ref[0], pk_ref[...])
    row = lax.broadcasted_iota(jnp.int32, (FOX_AUG, FOX_TILE), 0)
    ones_row = jnp.where(row == 0, 1.0, 0.0).astype(BF16)
    for h in range(N_MIX_HEADS):
        hs = slice(h * HEAD_DIM, (h + 1) * HEAD_DIM)
        for t in range(ROW_TILE // FOX_TILE):
            ts = slice(t * FOX_TILE, (t + 1) * FOX_TILE)
            qt_ref[0, h, t, :HEAD_DIM, :] = qt[hs, ts].astype(BF16)
            qt_ref[0, h, t, HEAD_DIM:, :] = aug[hs, ts].astype(BF16)
            vt_ref[0, h, t, :HEAD_DIM, :] = vt[hs, ts].astype(BF16)
            vt_ref[0, h, t, HEAD_DIM:, :] = ones_row
        k_ref[0, h] = kk[:, h * LANES:(h + 1) * LANES].astype(BF16)
    qm_ref[...] = _dot(xb, wqm_ref[...]).astype(BF16)


def _proj_b(x, ct, cs, wqt, wk, wvt, wqm, pq, pk, batch, seq):
    per_seq = seq // ROW_TILE
    tiles = ROW_TILE // FOX_TILE
    return pl.pallas_call(
        _proj_b_kernel,
        grid=(batch, per_seq),
        in_specs=[pl.BlockSpec((ROW_TILE, D_MODEL), lambda b, i: (b * per_seq + i, 0)),
                  pl.BlockSpec((1, LANES, ROW_TILE), lambda b, i: (b, 0, i)),
                  pl.BlockSpec((1, ROW_TILE, LANES), lambda b, i: (b, i, 0)),
                  _resident(wqt.shape), _resident(wk.shape), _resident(wvt.shape), _resident(wqm.shape),
                  _resident(pq.shape), _resident(pk.shape)],
        out_specs=[pl.BlockSpec((1, N_MIX_HEADS, tiles, LANES, FOX_TILE), lambda b, i: (b, 0, i, 0, 0)),
                   pl.BlockSpec((1, N_MIX_HEADS, ROW_TILE, LANES), lambda b, i: (b, 0, i, 0)),
                   pl.BlockSpec((1, N_MIX_HEADS, tiles, HEAD_DIM + FOX_AUG, FOX_TILE), lambda b, i: (b, 0, i, 0, 0)),
                   pl.BlockSpec((ROW_TILE, MEM_W), lambda b, i: (b * per_seq + i, 0))],
        out_shape=[jax.ShapeDtypeStruct((batch, N_MIX_HEADS, seq // FOX_TILE, LANES, FOX_TILE), BF16),
                   jax.ShapeDtypeStruct((batch, N_MIX_HEADS, seq, LANES), BF16),
                   jax.ShapeDtypeStruct((batch, N_MIX_HEADS, seq // FOX_TILE, HEAD_DIM + FOX_AUG, FOX_TILE), BF16),
                   jax.ShapeDtypeStruct((batch * seq, MEM_W), BF16)],
        compiler_params=_cparams("parallel", "parallel"),
        name="proj_b",
    )(x, ct, cs, wqt, wk, wvt, wqm, pq, pk)


def _fox_kernel(qt_ref, k_ref, vt_ref, o_ref):
    n_tiles = k_ref.shape[2] // FOX_TILE
    heads = k_ref.shape[1]
    key_pos = lax.broadcasted_iota(jnp.int32, (FOX_TILE, FOX_TILE), 0)
    qry_pos = lax.broadcasted_iota(jnp.int32, (FOX_TILE, FOX_TILE), 1)
    future = key_pos > qry_pos

    def q_tile(qi, _):
        qs = [qt_ref[0, h, qi] for h in range(heads)]

        def step(kj, carry, diagonal):
            ks = pl.multiple_of(kj * FOX_TILE, FOX_TILE)
            new = []
            for h in range(heads):
                m, acc = carry[h]
                s = _dot(k_ref[0, h, pl.ds(ks, FOX_TILE), :], qs[h])
                if diagonal:
                    s = jnp.where(future, NEG, s)
                m_new = jnp.maximum(m, jnp.max(s, axis=0, keepdims=True))
                p = jnp.exp(s - m_new)
                acc = acc * jnp.exp(m - m_new) + _dot(vt_ref[0, h, kj], p.astype(BF16))
                new.append((m_new, acc))
            return tuple(new)

        init = tuple((jnp.full((1, FOX_TILE), NEG, F32), jnp.zeros((HEAD_DIM + FOX_AUG, FOX_TILE), F32))
                     for _ in range(heads))
        carry = lax.fori_loop(0, qi, lambda kj, c: step(kj, c, False), init)
        carry = step(qi, carry, True)
        ot = jnp.concatenate([acc[:HEAD_DIM] / acc[HEAD_DIM:HEAD_DIM + 1] for _, acc in carry], axis=0)
        o_ref[0, pl.ds(pl.multiple_of(qi * FOX_TILE, FOX_TILE), FOX_TILE), :] = ot.T.astype(BF16)
        return 0

    lax.fori_loop(0, n_tiles, q_tile, 0)


def _fox(qt, kk, vt, batch, seq):
    pair = LANES // HEAD_DIM
    tiles = seq // FOX_TILE
    return pl.pallas_call(
        _fox_kernel,
        grid=(batch, N_MIX_HEADS // pair),
        in_specs=[pl.BlockSpec((1, pair, tiles, LANES, FOX_TILE), lambda b, h: (b, h, 0, 0, 0)),
                  pl.BlockSpec((1, pair, seq, LANES), lambda b, h: (b, h, 0, 0)),
                  pl.BlockSpec((1, pair, tiles, HEAD_DIM + FOX_AUG, FOX_TILE), lambda b, h: (b, h, 0, 0, 0))],
        out_specs=pl.BlockSpec((1, seq, LANES), lambda b, h: (b, 0, h)),
        out_shape=jax.ShapeDtypeStruct((batch, seq, MIX_W), BF16),
        compiler_params=_cparams("parallel", "parallel"),
        name="fox_attn",
    )(qt, kk, vt)


def _out_b_kernel(x_ref, o_ref, qm_ref, kv_ref, w_ref, g_ref, b_ref, y_ref):
    o_m = _memory_heads(qm_ref[...], kv_ref)
    mix = _dot(o_ref[...], w_ref[:MIX_W, :]) + _dot(o_m.astype(BF16), w_ref[MIX_W:, :])
    y_ref[...] = _layer_norm(DEEPNORM_ALPHA * x_ref[...] + mix, g_ref[...], b_ref[...])


def _out_b(x, o, qm, kv, w, g, b, seq):
    n = x.shape[0]
    per_seq = seq // ROW_TILE
    row = lambda w_: pl.BlockSpec((ROW_TILE, w_), lambda i: (i, 0))
    return pl.pallas_call(
        _out_b_kernel,
        grid=(n // ROW_TILE,),
        in_specs=[row(D_MODEL), row(MIX_W), row(MEM_W),
                  pl.BlockSpec((1, kv.shape[1], 2 * MEM_W), lambda i: (i // per_seq, 0, 0)),
                  _resident((MIX_W + MEM_W, D_MODEL)), _resident((1, D_MODEL)), _resident((1, D_MODEL))],
        out_specs=row(D_MODEL),
        out_shape=jax.ShapeDtypeStruct((n, D_MODEL), F32),
        compiler_params=_cparams("parallel"),
        name="out_b_ln",
    )(x, o, qm, kv, w, g, b)


def _rope_tables(seq):
    pos = jnp.arange(seq, dtype=F32)
    inv_freq = 1.0 / (ROPE_THETA ** (jnp.arange(ROT_HALF, dtype=F32) / ROT_HALF))
    ang = pos[:, None] * inv_freq[None, :]
    cos, sin = jnp.cos(ang), jnp.sin(ang)
    ones = jnp.ones((seq, HEAD_DIM - ROT_DIM), F32)
    zeros = jnp.zeros((seq, HEAD_DIM - ROT_DIM), F32)
    zero_half = jnp.zeros((seq, ROT_HALF), F32)
    head = lambda parts: jnp.concatenate(parts * (LANES // HEAD_DIM), axis=1)
    return head([cos, cos, ones]), head([-sin, zero_half, zeros]), head([zero_half, sin, zeros])


def _placement_matrices():
    h = jnp.arange(N_MIX_HEADS)
    pq = jnp.zeros((MIX_W, LANES), F32)
    pk = jnp.zeros((LANES, N_MIX_HEADS * LANES), F32)
    for part in range(3):
        pq = pq.at[h * HEAD_DIM + part, part * FOX_AUG + h].set(1.0)
        pq = pq.at[h * HEAD_DIM + 3 + part, 3 * FOX_AUG].set(1.0)
        pk = pk.at[3 * FOX_AUG, h * LANES + HEAD_DIM + part].set(1.0)
        pk = pk.at[part * FOX_AUG + h, h * LANES + HEAD_DIM + 3 + part].set(-1.0)
    return pq.astype(BF16), pk.astype(BF16)


def kernel(x, mem, ffn1_w_gate_up, ffn1_w_down, ffn2_w_gate_up, ffn2_w_down, ln_gain, ln_bias, mem_w_kv,
           a_w_in, a_w_out, b_w_in, b_forget_bias, b_w_out):
    batch, seq, _ = x.shape
    n = batch * seq
    assert seq % ROW_TILE == 0 and all(seq % (d * DIL_TILE) == 0 for _, d in DILATED_GROUPS)
    xf = x.reshape(n, D_MODEL)
    mem2d = mem.reshape(batch * mem.shape[1], D_MODEL)
    gain = lambda i, j: ln_gain[i, j].reshape(1, D_MODEL)
    bias = lambda i, j: ln_bias[i, j].reshape(1, D_MODEL)
    ffn = lambda v, wgu, wd, i, j: _ffn(v, wgu.astype(BF16), wd.astype(BF16), gain(i, j), bias(i, j))

    xf = ffn(xf, ffn1_w_gate_up[0], ffn1_w_down[0], 0, 0)
    wa = a_w_in[0]
    cols = []
    for g in range(len(DILATED_GROUPS)):
        cols += [wa[:, part * MIX_W + g * GROUP_W:part * MIX_W + (g + 1) * GROUP_W] * (ATTN_SCALE if part == 0 else 1.0)
                 for part in range(3)]
    cols.append(wa[:, 3 * MIX_W:] * ATTN_SCALE)
    w_in_a = jnp.concatenate(cols, axis=1).astype(BF16)
    ca, cm, cp = _rope_tables(seq)
    qkv0, qkv1, qkv2, qm = _proj_a(xf, w_in_a, ca, cm, cp, seq)
    os, lses = [], []
    for qkv, (_, dilation) in zip((qkv0, qkv1, qkv2), DILATED_GROUPS):
        o, lse = _dilated_group(qkv, batch, seq, dilation)
        os.append(o)
        lses.append(lse)
    kv = _memkv(mem2d, mem_w_kv[0].astype(BF16)).reshape(batch, mem.shape[1], 2 * MEM_W)
    xf = _out_a(xf, os, lses, qm, kv, a_w_out[0].astype(BF16), gain(0, 1), bias(0, 1), seq)
    xf = ffn(xf, ffn2_w_gate_up[0], ffn2_w_down[0], 0, 2)

    xf = ffn(xf, ffn1_w_gate_up[1], ffn1_w_down[1], 1, 0)
    wb = b_w_in[0]
    wf = jnp.pad(wb[:, 3 * MIX_W:3 * MIX_W + N_MIX_HEADS], ((0, 0), (0, LANES - N_MIX_HEADS))).astype(BF16)
    fb = jnp.pad(b_forget_bias[0].astype(F32), (0, FOX_AUG - N_MIX_HEADS)).reshape(FOX_AUG, 1)
    ct, cs = _decay(xf, wf, fb, batch, seq)
    wqt = (wb[:, :MIX_W] * ATTN_SCALE).T.astype(BF16)
    wk = jnp.pad(wb[:, MIX_W:2 * MIX_W].reshape(D_MODEL, N_MIX_HEADS, HEAD_DIM),
                 ((0, 0), (0, 0), (0, LANES - HEAD_DIM))).reshape(D_MODEL, N_MIX_HEADS * LANES).astype(BF16)
    wvt = wb[:, 2 * MIX_W:3 * MIX_W].T.astype(BF16)
    wqm = (wb[:, 3 * MIX_W + N_MIX_HEADS:] * ATTN_SCALE).astype(BF16)
    pq, pk = _placement_matrices()
    qt, kk, vt, qm = _proj_b(xf, ct, cs, wqt, wk, wvt, wqm, pq, pk, batch, seq)
    o = _fox(qt, kk, vt, batch, seq).reshape(n, MIX_W)
    kv = _memkv(mem2d, mem_w_kv[1].astype(BF16)).reshape(batch, mem.shape[1], 2 * MEM_W)
    xf = _out_b(xf, o, qm, kv, b_w_out[0].astype(BF16), gain(1, 1), bias(1, 1), seq)
    xf = ffn(xf, ffn2_w_gate_up[1], ffn2_w_down[1], 1, 2)
    return xf.reshape(batch, seq, D_MODEL)
```

```python
import functools

import jax
import jax.numpy as jnp
from jax import lax
from jax.experimental import pallas as pl
from jax.experimental.pallas import tpu as pltpu

D_MODEL = 1024
DEPTH = 2
HEAD_DIM = 64
N_MIX_HEADS = 12
N_MEM_HEADS = 4
DILATED_GROUPS = ((128, 1), (512, 4), (2048, 16))
HEADS_PER_GROUP = N_MIX_HEADS // len(DILATED_GROUPS)
ROT_DIM = HEAD_DIM // 4
ROT_HALF = ROT_DIM // 2
ROPE_THETA = 500000.0
D_FF = 2816
BAND = 128
DEEPNORM_ALPHA = (2 * DEPTH) ** 0.25
LN_EPS = 1e-5
MIX_W = N_MIX_HEADS * HEAD_DIM
MEM_W = N_MEM_HEADS * HEAD_DIM
GROUP_W = HEADS_PER_GROUP * HEAD_DIM
ATTN_SCALE = HEAD_DIM ** -0.5

LANES = 128
VMEM_LIMIT = 56 * 1024 * 1024

ROW_TILE = 512
FF_CHUNK = 256
FOX_TILE = 512
FOX_AUG = 16
DIL_TILE = 256

F32 = jnp.float32
BF16 = jnp.bfloat16
NEG = -1e30


def _cparams(*sem):
    return pltpu.CompilerParams(dimension_semantics=sem, vmem_limit_bytes=VMEM_LIMIT)


def _resident(shape):
    return pl.BlockSpec(shape, lambda *_: (0,) * len(shape), pipeline_mode=pl.Buffered(1))


def _dot(a, b):
    return jnp.dot(a, b, preferred_element_type=F32)


def _dot_nt(a, b):
    return lax.dot_general(a, b, (((1,), (1,)), ((), ())), preferred_element_type=F32)


def _layer_norm(z, g, b):
    mu = jnp.mean(z, axis=-1, keepdims=True)
    zc = z - mu
    var = jnp.mean(zc * zc, axis=-1, keepdims=True)
    return zc * lax.rsqrt(var + LN_EPS) * g + b


def _ffn_kernel(x_ref, wgu_ref, wd_ref, g_ref, b_ref, o_ref, h_ref):
    x = x_ref[...]
    xb = x.astype(BF16)
    for c in range(D_FF // FF_CHUNK):
        lo = c * FF_CHUNK
        gate = _dot(xb, wgu_ref[:, lo:lo + FF_CHUNK])
        up = _dot(xb, wgu_ref[:, D_FF + lo:D_FF + lo + FF_CHUNK])
        h_ref[:, lo:lo + FF_CHUNK] = (gate * jax.nn.sigmoid(gate) * up).astype(BF16)
    y = _dot(h_ref[...], wd_ref[...])
    o_ref[...] = _layer_norm(DEEPNORM_ALPHA * x + 0.5 * y, g_ref[...], b_ref[...])


def _ffn(x, wgu, wd, g, b):
    n = x.shape[0]
    row = pl.BlockSpec((ROW_TILE, D_MODEL), lambda i: (i, 0))
    return pl.pallas_call(
        _ffn_kernel,
        grid=(n // ROW_TILE,),
        in_specs=[row, _resident((D_MODEL, 2 * D_FF)), _resident((D_FF, D_MODEL)),
                  _resident((1, D_MODEL)), _resident((1, D_MODEL))],
        out_specs=row,
        out_shape=jax.ShapeDtypeStruct((n, D_MODEL), F32),
        scratch_shapes=[pltpu.VMEM((ROW_TILE, D_FF), BF16)],
        compiler_params=_cparams("parallel"),
        name="ffn_ln",
    )(x, wgu, wd, g, b)


def _memkv_kernel(m_ref, w_ref, o_ref):
    o_ref[...] = _dot(m_ref[...].astype(BF16), w_ref[...]).astype(BF16)


def _memkv(mem2d, w):
    n = mem2d.shape[0]
    tile = min(ROW_TILE, n)
    return pl.pallas_call(
        _memkv_kernel,
        grid=(n // tile,),
        in_specs=[pl.BlockSpec((tile, D_MODEL), lambda i: (i, 0)), _resident((D_MODEL, 2 * MEM_W))],
        out_specs=pl.BlockSpec((tile, 2 * MEM_W), lambda i: (i, 0)),
        out_shape=jax.ShapeDtypeStruct((n, 2 * MEM_W), BF16),
        compiler_params=_cparams("parallel"),
        name="mem_kv",
    )(mem2d, w)


def _memory_heads(qm, kv_ref):
    outs = []
    for h in range(N_MEM_HEADS):
        sl = slice(h * HEAD_DIM, (h + 1) * HEAD_DIM)
        km = kv_ref[0, :, h * HEAD_DIM:(h + 1) * HEAD_DIM]
        vm = kv_ref[0, :, MEM_W + h * HEAD_DIM:MEM_W + (h + 1) * HEAD_DIM]
        s = _dot_nt(qm[:, sl], km)
        p = jnp.exp(s - jnp.max(s, axis=-1, keepdims=True))
        l = jnp.sum(p, axis=-1, keepdims=True)
        outs.append(_dot(p.astype(BF16), vm) / l)
    return jnp.concatenate(outs, axis=-1)


def _proj_a_kernel(x_ref, w_ref, ca_ref, cm_ref, cp_ref, o0_ref, o1_ref, o2_ref, qm_ref):
    xb = x_ref[...].astype(BF16)
    ca, cm, cp = ca_ref[...], cm_ref[...], cp_ref[...]
    qk_w = 2 * GROUP_W
    for g, o_ref in enumerate((o0_ref, o1_ref, o2_ref)):
        h = _dot(xb, w_ref[:, g * 3 * GROUP_W:(g + 1) * 3 * GROUP_W])
        for cb in range(qk_w // LANES):
            t = h[:, cb * LANES:(cb + 1) * LANES]
            r = t * ca + pltpu.roll(t, LANES - ROT_HALF, 1) * cm + pltpu.roll(t, ROT_HALF, 1) * cp
            o_ref[:, cb * LANES:(cb + 1) * LANES] = r.astype(BF16)
        o_ref[:, qk_w:] = h[:, qk_w:].astype(BF16)
    qm_ref[...] = _dot(xb, w_ref[:, 3 * MIX_W:]).astype(BF16)


def _proj_a(x, w, ca, cm, cp, seq):
    n = x.shape[0]
    per_seq = seq // ROW_TILE
    row = lambda w_: pl.BlockSpec((ROW_TILE, w_), lambda i: (i, 0))
    tab = pl.BlockSpec((ROW_TILE, LANES), lambda i: (i % per_seq, 0))
    return pl.pallas_call(
        _proj_a_kernel,
        grid=(n // ROW_TILE,),
        in_specs=[row(D_MODEL), _resident((D_MODEL, 3 * MIX_W + MEM_W)), tab, tab, tab],
        out_specs=[row(3 * GROUP_W)] * 3 + [row(MEM_W)],
        out_shape=[jax.ShapeDtypeStruct((n, 3 * GROUP_W), BF16)] * 3 + [jax.ShapeDtypeStruct((n, MEM_W), BF16)],
        compiler_params=_cparams("parallel"),
        name="proj_a",
    )(x, w, ca, cm, cp)


def _dilated_kernel(cur_ref, prev_ref, o_ref, lse_ref):
    first_key = jnp.where(pl.program_id(2) == 0, BAND, 0)
    qi = lax.broadcasted_iota(jnp.int32, (BAND, 2 * BAND), 0)
    ki = lax.broadcasted_iota(jnp.int32, (BAND, 2 * BAND), 1)
    band = (ki >= qi) & (ki <= qi + BAND)
    for sb in range(DIL_TILE // BAND):
        rows = slice(sb * BAND, (sb + 1) * BAND)
        if sb == 0:
            before = prev_ref[0]
            mask = band & (ki >= first_key)
        else:
            before = cur_ref[0, (sb - 1) * BAND:sb * BAND, :]
            mask = band
        kv = jnp.concatenate([before, cur_ref[0, rows, :]], axis=0)
        outs, lses = [], []
        for h in range(HEADS_PER_GROUP):
            q = cur_ref[0, rows, h * HEAD_DIM:(h + 1) * HEAD_DIM]
            k = kv[:, GROUP_W + h * HEAD_DIM:GROUP_W + (h + 1) * HEAD_DIM]
            v = kv[:, 2 * GROUP_W + h * HEAD_DIM:2 * GROUP_W + (h + 1) * HEAD_DIM]
            s = jnp.where(mask, _dot_nt(q, k), NEG)
            m = jnp.max(s, axis=-1, keepdims=True)
            p = jnp.exp(s - m)
            l = jnp.sum(p, axis=-1, keepdims=True)
            outs.append(_dot(p.astype(BF16), v) / l)
            lses.append(jnp.broadcast_to(m + jnp.log(l), (BAND, HEAD_DIM)))
        o_ref[0, rows, :] = jnp.concatenate(outs, axis=-1)
        lse_ref[0, rows, :] = jnp.concatenate(lses, axis=-1)


def _dilated_group(qkv, batch, seq, dilation):
    length = seq // dilation
    tile = min(DIL_TILE, length)
    assert tile == DIL_TILE and length % tile == 0
    per = tile // BAND
    view = qkv.reshape(batch, length, dilation * 3 * GROUP_W)
    out_block = pl.BlockSpec((1, tile, GROUP_W), lambda b, j, t: (b, t, j))
    o, lse = pl.pallas_call(
        _dilated_kernel,
        grid=(batch, dilation, length // tile),
        in_specs=[pl.BlockSpec((1, tile, 3 * GROUP_W), lambda b, j, t: (b, t, j)),
                  pl.BlockSpec((1, BAND, 3 * GROUP_W), lambda b, j, t: (b, jnp.maximum(t * per - 1, 0), j))],
        out_specs=[out_block, out_block],
        out_shape=[jax.ShapeDtypeStruct((batch, length, dilation * GROUP_W), F32)] * 2,
        compiler_params=_cparams("parallel", "parallel", "parallel"),
        name=f"dilated_attn_d{dilation}",
    )(view, view)
    return o.reshape(batch * seq, GROUP_W), lse.reshape(batch * seq, GROUP_W)


def _out_a_kernel(x_ref, o0_ref, o1_ref, o2_ref, l0_ref, l1_ref, l2_ref, qm_ref, kv_ref, w_ref, g_ref, b_ref,
                  y_ref):
    l0, l1, l2 = l0_ref[...], l1_ref[...], l2_ref[...]
    m = jnp.maximum(jnp.maximum(l0, l1), l2)
    e0, e1, e2 = jnp.exp(l0 - m), jnp.exp(l1 - m), jnp.exp(l2 - m)
    o_a = (e0 * o0_ref[...] + e1 * o1_ref[...] + e2 * o2_ref[...]) / (e0 + e1 + e2)
    o_m = _memory_heads(qm_ref[...], kv_ref)
    mix = _dot(o_a.astype(BF16), w_ref[:GROUP_W, :]) + _dot(o_m.astype(BF16), w_ref[GROUP_W:, :])
    y_ref[...] = _layer_norm(DEEPNORM_ALPHA * x_ref[...] + mix, g_ref[...], b_ref[...])


def _out_a(x, os, lses, qm, kv, w, g, b, seq):
    n = x.shape[0]
    per_seq = seq // ROW_TILE
    row = lambda w_: pl.BlockSpec((ROW_TILE, w_), lambda i: (i, 0))
    return pl.pallas_call(
        _out_a_kernel,
        grid=(n // ROW_TILE,),
        in_specs=[row(D_MODEL)] + [row(GROUP_W)] * 6 + [row(MEM_W),
                  pl.BlockSpec((1, kv.shape[1], 2 * MEM_W), lambda i: (i // per_seq, 0, 0)),
                  _resident((GROUP_W + MEM_W, D_MODEL)), _resident((1, D_MODEL)), _resident((1, D_MODEL))],
        out_specs=row(D_MODEL),
        out_shape=jax.ShapeDtypeStruct((n, D_MODEL), F32),
        compiler_params=_cparams("parallel"),
        name="out_a_ln",
    )(x, *os, *lses, qm, kv, w, g, b)


def _decay_kernel(x_ref, wf_ref, fb_ref, ct_ref, cs_ref, carry_ref):
    @pl.when(pl.program_id(1) == 0)
    def _():
        carry_ref[...] = jnp.zeros_like(carry_ref)

    f = _dot(x_ref[...].astype(BF16), wf_ref[...])
    ft = f.T[:FOX_AUG] + fb_ref[...]
    logf = jnp.minimum(ft, 0.0) - jnp.log1p(jnp.exp(-jnp.abs(ft)))
    lane = lax.broadcasted_iota(jnp.int32, logf.shape, 1)
    c = logf
    shift = 1
    while shift < ROW_TILE:
        c = c + jnp.where(lane >= shift, pltpu.roll(c, shift, 1), 0.0)
        shift *= 2
    c = c + carry_ref[...]
    carry_ref[...] = c[:, ROW_TILE - 1:ROW_TILE]
    hi = c.astype(BF16)
    r1 = c - hi.astype(F32)
    mid = r1.astype(BF16)
    lo = (r1 - mid.astype(F32)).astype(BF16)
    ones = jnp.ones((FOX_AUG, ROW_TILE), BF16)
    zeros = jnp.zeros((LANES - 4 * FOX_AUG, ROW_TILE), BF16)
    stack = jnp.concatenate([hi, mid, lo, ones, zeros], axis=0)
    ct_ref[0] = stack
    cs_ref[0] = stack.astype(F32).T.astype(BF16)


def _decay(x, wf, fb, batch, seq):
    per_seq = seq // ROW_TILE
    return pl.pallas_call(
        _decay_kernel,
        grid=(batch, per_seq),
        in_specs=[pl.BlockSpec((ROW_TILE, D_MODEL), lambda b, i: (b * per_seq + i, 0)),
                  _resident((D_MODEL, LANES)), _resident((FOX_AUG, 1))],
        out_specs=[pl.BlockSpec((1, LANES, ROW_TILE), lambda b, i: (b, 0, i)),
                   pl.BlockSpec((1, ROW_TILE, LANES), lambda b, i: (b, i, 0))],
        out_shape=[jax.ShapeDtypeStruct((batch, LANES, seq), BF16), jax.ShapeDtypeStruct((batch, seq, LANES), BF16)],
        scratch_shapes=[pltpu.VMEM((FOX_AUG, 1), F32)],
        compiler_params=_cparams("parallel", "arbitrary"),
        name="decay_cumsum",
    )(x, wf, fb)


def _proj_b_kernel(x_ref, ct_ref, cs_ref, wqt_ref, wk_ref, wvt_ref, wqm_ref, pq_ref, pk_ref,
                   qt_ref, k_ref, vt_ref, qm_ref):
    xb = x_ref[...].astype(BF16)
    qt = _dot_nt(wqt_ref[...], xb)
    aug = _dot(pq_ref[...], ct_ref[0])
    vt = _dot_nt(wvt_ref[...], xb)
    kk = _dot(xb, wk_ref[...]) + _dot(cs_ref[0], pk_ref[...])
    row = lax.broadcasted_iota(jnp.int32, (FOX_AUG, ROW_TILE), 0)
    ones_row = jnp.where(row == 0, 1.0, 0.0).astype(BF16)
    for h in range(N_MIX_HEADS):
        hs = slice(h * HEAD_DIM, (h + 1) * HEAD_DIM)
        qt_ref[0, h, :HEAD_DIM, :] = qt[hs].astype(BF16)
        qt_ref[0, h, HEAD_DIM:, :] = aug[hs].astype(BF16)
        vt_ref[0, h, :HEAD_DIM, :] = vt[hs].astype(BF16)
        vt_ref[0, h, HEAD_DIM:, :] = ones_row
        k_ref[0, h] = kk[:, h * LANES:(h + 1) * LANES].astype(BF16)
    qm_ref[...] = _dot(xb, wqm_ref[...]).astype(BF16)


def _proj_b(x, ct, cs, wqt, wk, wvt, wqm, pq, pk, batch, seq):
    per_seq = seq // ROW_TILE
    v_rows = HEAD_DIM + FOX_AUG
    return pl.pallas_call(
        _proj_b_kernel,
        grid=(batch, per_seq),
        in_specs=[pl.BlockSpec((ROW_TILE, D_MODEL), lambda b, i: (b * per_seq + i, 0)),
                  pl.BlockSpec((1, LANES, ROW_TILE), lambda b, i: (b, 0, i)),
                  pl.BlockSpec((1, ROW_TILE, LANES), lambda b, i: (b, i, 0)),
                  _resident(wqt.shape), _resident(wk.shape), _resident(wvt.shape), _resident(wqm.shape),
                  _resident(pq.shape), _resident(pk.shape)],
        out_specs=[pl.BlockSpec((1, N_MIX_HEADS, LANES, ROW_TILE), lambda b, i: (b, 0, 0, i)),
                   pl.BlockSpec((1, N_MIX_HEADS, ROW_TILE, LANES), lambda b, i: (b, 0, i, 0)),
                   pl.BlockSpec((1, N_MIX_HEADS, v_rows, ROW_TILE), lambda b, i: (b, 0, 0, i)),
                   pl.BlockSpec((ROW_TILE, MEM_W), lambda b, i: (b * per_seq + i, 0))],
        out_shape=[jax.ShapeDtypeStruct((batch, N_MIX_HEADS, LANES, seq), BF16),
                   jax.ShapeDtypeStruct((batch, N_MIX_HEADS, seq, LANES), BF16),
                   jax.ShapeDtypeStruct((batch, N_MIX_HEADS, v_rows, seq), BF16),
                   jax.ShapeDtypeStruct((batch * seq, MEM_W), BF16)],
        compiler_params=_cparams("parallel", "parallel"),
        name="proj_b",
    )(x, ct, cs, wqt, wk, wvt, wqm, pq, pk)


def _fox_kernel(qt_ref, k_ref, vt_ref, o_ref, s_ref, p_ref):
    seq = k_ref.shape[2]
    heads = k_ref.shape[1]
    key_pos = lax.broadcasted_iota(jnp.int32, (FOX_TILE, FOX_TILE), 0)
    qry_pos = lax.broadcasted_iota(jnp.int32, (FOX_TILE, FOX_TILE), 1)
    future = key_pos > qry_pos
    for qi in range(seq // FOX_TILE):
        cols = slice(qi * FOX_TILE, (qi + 1) * FOX_TILE)
        n_keys = (qi + 1) * FOX_TILE
        outs = []
        for h in range(heads):
            slot = (qi * heads + h) % 2
            q = qt_ref[0, h, :, cols]
            m = None
            for kc in range(qi + 1):
                rows = slice(kc * FOX_TILE, (kc + 1) * FOX_TILE)
                s = _dot(k_ref[0, h, rows, :], q)
                if kc == qi:
                    s = jnp.where(future, NEG, s)
                s_ref[slot, rows, :] = s
                mc = jnp.max(s, axis=0, keepdims=True)
                m = mc if m is None else jnp.maximum(m, mc)
            for kc in range(qi + 1):
                rows = slice(kc * FOX_TILE, (kc + 1) * FOX_TILE)
                p_ref[slot, rows, :] = jnp.exp(s_ref[slot, rows, :] - m).astype(BF16)
            acc = _dot(vt_ref[0, h, :, :n_keys], p_ref[slot, :n_keys, :])
            outs.append(acc[:HEAD_DIM] / acc[HEAD_DIM:HEAD_DIM + 1])
        o_ref[0, cols, :] = jnp.concatenate(outs, axis=0).T.astype(BF16)


def _fox(qt, kk, vt, batch, seq):
    pair = LANES // HEAD_DIM
    v_rows = HEAD_DIM + FOX_AUG
    return pl.pallas_call(
        _fox_kernel,
        grid=(batch, N_MIX_HEADS // pair),
        in_specs=[pl.BlockSpec((1, pair, LANES, seq), lambda b, h: (b, h, 0, 0)),
                  pl.BlockSpec((1, pair, seq, LANES), lambda b, h: (b, h, 0, 0)),
                  pl.BlockSpec((1, pair, v_rows, seq), lambda b, h: (b, h, 0, 0))],
        out_specs=pl.BlockSpec((1, seq, LANES), lambda b, h: (b, 0, h)),
        out_shape=jax.ShapeDtypeStruct((batch, seq, MIX_W), BF16),
        scratch_shapes=[pltpu.VMEM((2, seq, FOX_TILE), F32), pltpu.VMEM((2, seq, FOX_TILE), BF16)],
        compiler_params=_cparams("parallel", "parallel"),
        name="fox_attn",
    )(qt, kk, vt)


def _out_b_kernel(x_ref, o_ref, qm_ref, kv_ref, w_ref, g_ref, b_ref, y_ref):
    o_m = _memory_heads(qm_ref[...], kv_ref)
    mix = _dot(o_ref[...], w_ref[:MIX_W, :]) + _dot(o_m.astype(BF16), w_ref[MIX_W:, :])
    y_ref[...] = _layer_norm(DEEPNORM_ALPHA * x_ref[...] + mix, g_ref[...], b_ref[...])


def _out_b(x, o, qm, kv, w, g, b, seq):
    n = x.shape[0]
    per_seq = seq // ROW_TILE
    row = lambda w_: pl.BlockSpec((ROW_TILE, w_), lambda i: (i, 0))
    return pl.pallas_call(
        _out_b_kernel,
        grid=(n // ROW_TILE,),
        in_specs=[row(D_MODEL), row(MIX_W), row(MEM_W),
                  pl.BlockSpec((1, kv.shape[1], 2 * MEM_W), lambda i: (i // per_seq, 0, 0)),
                  _resident((MIX_W + MEM_W, D_MODEL)), _resident((1, D_MODEL)), _resident((1, D_MODEL))],
        out_specs=row(D_MODEL),
        out_shape=jax.ShapeDtypeStruct((n, D_MODEL), F32),
        compiler_params=_cparams("parallel"),
        name="out_b_ln",
    )(x, o, qm, kv, w, g, b)


def _rope_tables(seq):
    pos = jnp.arange(seq, dtype=F32)
    inv_freq = 1.0 / (ROPE_THETA ** (jnp.arange(ROT_HALF, dtype=F32) / ROT_HALF))
    ang = pos[:, None] * inv_freq[None, :]
    cos, sin = jnp.cos(ang), jnp.sin(ang)
    ones = jnp.ones((seq, HEAD_DIM - ROT_DIM), F32)
    zeros = jnp.zeros((seq, HEAD_DIM - ROT_DIM), F32)
    zero_half = jnp.zeros((seq, ROT_HALF), F32)
    head = lambda parts: jnp.concatenate(parts * (LANES // HEAD_DIM), axis=1)
    return head([cos, cos, ones]), head([-sin, zero_half, zeros]), head([zero_half, sin, zeros])


def _placement_matrices():
    h = jnp.arange(N_MIX_HEADS)
    pq = jnp.zeros((MIX_W, LANES), F32)
    pk = jnp.zeros((LANES, N_MIX_HEADS * LANES), F32)
    for part in range(3):
        pq = pq.at[h * HEAD_DIM + part, part * FOX_AUG + h].set(1.0)
        pq = pq.at[h * HEAD_DIM + 3 + part, 3 * FOX_AUG].set(1.0)
        pk = pk.at[3 * FOX_AUG, h * LANES + HEAD_DIM + part].set(1.0)
        pk = pk.at[part * FOX_AUG + h, h * LANES + HEAD_DIM + 3 + part].set(-1.0)
    return pq.astype(BF16), pk.astype(BF16)


def kernel(x, mem, ffn1_w_gate_up, ffn1_w_down, ffn2_w_gate_up, ffn2_w_down, ln_gain, ln_bias, mem_w_kv,
           a_w_in, a_w_out, b_w_in, b_forget_bias, b_w_out):
    batch, seq, _ = x.shape
    n = batch * seq
    assert seq % ROW_TILE == 0 and all(seq % (d * DIL_TILE) == 0 for _, d in DILATED_GROUPS)
    xf = x.reshape(n, D_MODEL)
    mem2d = mem.reshape(batch * mem.shape[1], D_MODEL)
    gain = lambda i, j: ln_gain[i, j].reshape(1, D_MODEL)
    bias = lambda i, j: ln_bias[i, j].reshape(1, D_MODEL)
    ffn = lambda v, wgu, wd, i, j: _ffn(v, wgu.astype(BF16), wd.astype(BF16), gain(i, j), bias(i, j))

    xf = ffn(xf, ffn1_w_gate_up[0], ffn1_w_down[0], 0, 0)
    wa = a_w_in[0]
    cols = []
    for g in range(len(DILATED_GROUPS)):
        cols += [wa[:, part * MIX_W + g * GROUP_W:part * MIX_W + (g + 1) * GROUP_W] * (ATTN_SCALE if part == 0 else 1.0)
                 for part in range(3)]
    cols.append(wa[:, 3 * MIX_W:] * ATTN_SCALE)
    w_in_a = jnp.concatenate(cols, axis=1).astype(BF16)
    ca, cm, cp = _rope_tables(seq)
    qkv0, qkv1, qkv2, qm = _proj_a(xf, w_in_a, ca, cm, cp, seq)
    os, lses = [], []
    for qkv, (_, dilation) in zip((qkv0, qkv1, qkv2), DILATED_GROUPS):
        o, lse = _dilated_group(qkv, batch, seq, dilation)
        os.append(o)
        lses.append(lse)
    kv = _memkv(mem2d, mem_w_kv[0].astype(BF16)).reshape(batch, mem.shape[1], 2 * MEM_W)
    xf = _out_a(xf, os, lses, qm, kv, a_w_out[0].astype(BF16), gain(0, 1), bias(0, 1), seq)
    xf = ffn(xf, ffn2_w_gate_up[0], ffn2_w_down[0], 0, 2)

    xf = ffn(xf, ffn1_w_gate_up[1], ffn1_w_down[1], 1, 0)
    wb = b_w_in[0]
    wf = jnp.pad(wb[:, 3 * MIX_W:3 * MIX_W + N_MIX_HEADS], ((0, 0), (0, LANES - N_MIX_HEADS))).astype(BF16)
    fb = jnp.pad(b_forget_bias[0].astype(F32), (0, FOX_AUG - N_MIX_HEADS)).reshape(FOX_AUG, 1)
    ct, cs = _decay(xf, wf, fb, batch, seq)
    wqt = (wb[:, :MIX_W] * ATTN_SCALE).T.astype(BF16)
    wk = jnp.pad(wb[:, MIX_W:2 * MIX_W].reshape(D_MODEL, N_MIX_HEADS, HEAD_DIM),
                 ((0, 0), (0, 0), (0, LANES - HEAD_DIM))).reshape(D_MODEL, N_MIX_HEADS * LANES).astype(BF16)
    wvt = wb[:, 2 * MIX_W:3 * MIX_W].T.astype(BF16)
    wqm = (wb[:, 3 * MIX_W + N_MIX_HEADS:] * ATTN_SCALE).astype(BF16)
    pq, pk = _placement_matrices()
    qt, kk, vt, qm = _proj_b(xf, ct, cs, wqt, wk, wvt, wqm, pq, pk, batch, seq)
    o = _fox(qt, kk, vt, batch, seq).reshape(n, MIX_W)
    kv = _memkv(mem2d, mem_w_kv[1].astype(BF16)).reshape(batch, mem.shape[1], 2 * MEM_W)
    xf = _out_b(xf, o, qm, kv, b_w_out[0].astype(BF16), gain(1, 1), bias(1, 1), seq)
    xf = ffn(xf, ffn2_w_gate_up[1], ffn2_w_down[1], 1, 2)
    return xf.reshape(batch, seq, D_MODEL)
```

```python
import functools

import jax
import jax.numpy as jnp
from jax import lax
from jax.experimental import pallas as pl
from jax.experimental.pallas import tpu as pltpu

D_MODEL = 1024
DEPTH = 2
HEAD_DIM = 64
N_MIX_HEADS = 12
N_MEM_HEADS = 4
DILATED_GROUPS = ((128, 1), (512, 4), (2048, 16))
HEADS_PER_GROUP = N_MIX_HEADS // len(DILATED_GROUPS)
ROT_DIM = HEAD_DIM // 4
ROT_HALF = ROT_DIM // 2
ROPE_THETA = 500000.0
D_FF = 2816
BAND = 128
DEEPNORM_ALPHA = (2 * DEPTH) ** 0.25
LN_EPS = 1e-5
MIX_W = N_MIX_HEADS * HEAD_DIM
MEM_W = N_MEM_HEADS * HEAD_DIM
GROUP_W = HEADS_PER_GROUP * HEAD_DIM
ATTN_SCALE = HEAD_DIM ** -0.5

LANES = 128
VMEM_LIMIT = 56 * 1024 * 1024

ROW_TILE = 512
FF_CHUNK = 256
FOX_TILE = 512
FOX_AUG = 16
DIL_TILE = 256

F32 = jnp.float32
BF16 = jnp.bfloat16
NEG = -1e30


def _cparams(*sem):
    return pltpu.CompilerParams(dimension_semantics=sem, vmem_limit_bytes=VMEM_LIMIT)


def _resident(shape):
    return pl.BlockSpec(shape, lambda *_: (0,) * len(shape), pipeline_mode=pl.Buffered(1))


def _dot(a, b):
    return jnp.dot(a, b, preferred_element_type=F32)


def _dot_nt(a, b):
    return lax.dot_general(a, b, (((1,), (1,)), ((), ())), preferred_element_type=F32)


def _layer_norm(z, g, b):
    mu = jnp.mean(z, axis=-1, keepdims=True)
    zc = z - mu
    var = jnp.mean(zc * zc, axis=-1, keepdims=True)
    return zc * lax.rsqrt(var + LN_EPS) * g + b


def _ffn_kernel(x_ref, wgu_ref, wd_ref, g_ref, b_ref, o_ref, h_ref):
    half = ROW_TILE // 2
    for part in range(2):
        rows = slice(part * half, (part + 1) * half)
        x = x_ref[rows, :]
        xb = x.astype(BF16)
        for c in range(D_FF // FF_CHUNK):
            lo = c * FF_CHUNK
            gate = _dot(xb, wgu_ref[:, lo:lo + FF_CHUNK])
            up = _dot(xb, wgu_ref[:, D_FF + lo:D_FF + lo + FF_CHUNK])
            h_ref[rows, lo:lo + FF_CHUNK] = (gate * jax.nn.sigmoid(gate) * up).astype(BF16)
        y = _dot(h_ref[rows, :], wd_ref[...])
        o_ref[rows, :] = _layer_norm(DEEPNORM_ALPHA * x + 0.5 * y, g_ref[...], b_ref[...])


def _ffn(x, wgu, wd, g, b):
    n = x.shape[0]
    row = pl.BlockSpec((ROW_TILE, D_MODEL), lambda i: (i, 0))
    return pl.pallas_call(
        _ffn_kernel,
        grid=(n // ROW_TILE,),
        in_specs=[row, _resident((D_MODEL, 2 * D_FF)), _resident((D_FF, D_MODEL)),
                  _resident((1, D_MODEL)), _resident((1, D_MODEL))],
        out_specs=row,
        out_shape=jax.ShapeDtypeStruct((n, D_MODEL), F32),
        scratch_shapes=[pltpu.VMEM((ROW_TILE, D_FF), BF16)],
        compiler_params=_cparams("parallel"),
        name="ffn_ln",
    )(x, wgu, wd, g, b)


def _memkv_kernel(m_ref, w_ref, o_ref):
    o_ref[...] = _dot(m_ref[...].astype(BF16), w_ref[...]).astype(BF16)


def _memkv(mem2d, w):
    n = mem2d.shape[0]
    tile = min(ROW_TILE, n)
    return pl.pallas_call(
        _memkv_kernel,
        grid=(n // tile,),
        in_specs=[pl.BlockSpec((tile, D_MODEL), lambda i: (i, 0)), _resident((D_MODEL, 2 * MEM_W))],
        out_specs=pl.BlockSpec((tile, 2 * MEM_W), lambda i: (i, 0)),
        out_shape=jax.ShapeDtypeStruct((n, 2 * MEM_W), BF16),
        compiler_params=_cparams("parallel"),
        name="mem_kv",
    )(mem2d, w)


def _memory_scores(qm, kv_ref):
    return [_dot_nt(qm[:, h * HEAD_DIM:(h + 1) * HEAD_DIM], kv_ref[0, :, h * HEAD_DIM:(h + 1) * HEAD_DIM])
            for h in range(N_MEM_HEADS)]


def _memory_values(scores, kv_ref):
    outs = []
    for h, s in enumerate(scores):
        vm = kv_ref[0, :, MEM_W + h * HEAD_DIM:MEM_W + (h + 1) * HEAD_DIM]
        p = jnp.exp(s - jnp.max(s, axis=-1, keepdims=True))
        l = jnp.sum(p, axis=-1, keepdims=True)
        outs.append(_dot(p.astype(BF16), vm) / l)
    return jnp.concatenate(outs, axis=-1)


def _proj_a_kernel(x_ref, w_ref, ca_ref, cm_ref, cp_ref, o0_ref, o1_ref, o2_ref, qm_ref, stage_ref):
    xb = x_ref[...].astype(BF16)
    ca, cm, cp = ca_ref[...], cm_ref[...], cp_ref[...]
    rot_blocks = 2 * GROUP_W // LANES
    for g, o_ref in enumerate((o0_ref, o1_ref, o2_ref)):
        dilation = DILATED_GROUPS[g][1]
        per = ROW_TILE // dilation
        h = _dot(xb, w_ref[:, g * 3 * GROUP_W:(g + 1) * 3 * GROUP_W])
        for cb in range(3 * GROUP_W // LANES):
            t = h[:, cb * LANES:(cb + 1) * LANES]
            if cb < rot_blocks:
                t = t * ca + pltpu.roll(t, LANES - ROT_HALF, 1) * cm + pltpu.roll(t, ROT_HALF, 1) * cp
            if dilation == 1:
                o_ref[0, 0, :, cb * LANES:(cb + 1) * LANES] = t.astype(BF16)
            else:
                stage_ref[cb] = t
        if dilation > 1:
            for j in range(dilation):
                rows = [stage_ref[cb, pl.ds(j, per, stride=dilation), :] for cb in range(3 * GROUP_W // LANES)]
                o_ref[0, j] = jnp.concatenate(rows, axis=1).astype(BF16)
    qm_ref[...] = _dot(xb, w_ref[:, 3 * MIX_W:]).astype(BF16)


def _proj_a(x, w, ca, cm, cp, batch, seq):
    per_seq = seq // ROW_TILE
    tab = pl.BlockSpec((ROW_TILE, LANES), lambda b, i: (i, 0))
    grouped = lambda d: pl.BlockSpec((1, d, ROW_TILE // d, 3 * GROUP_W), lambda b, i: (b, 0, i, 0))
    return pl.pallas_call(
        _proj_a_kernel,
        grid=(batch, per_seq),
        in_specs=[pl.BlockSpec((ROW_TILE, D_MODEL), lambda b, i: (b * per_seq + i, 0)),
                  _resident((D_MODEL, 3 * MIX_W + MEM_W)), tab, tab, tab],
        out_specs=[grouped(d) for _, d in DILATED_GROUPS]
                  + [pl.BlockSpec((ROW_TILE, MEM_W), lambda b, i: (b * per_seq + i, 0))],
        out_shape=[jax.ShapeDtypeStruct((batch, d, seq // d, 3 * GROUP_W), BF16) for _, d in DILATED_GROUPS]
                  + [jax.ShapeDtypeStruct((batch * seq, MEM_W), BF16)],
        scratch_shapes=[pltpu.VMEM((3 * GROUP_W // LANES, ROW_TILE, LANES), F32)],
        compiler_params=_cparams("parallel", "parallel"),
        name="proj_a",
    )(x, w, ca, cm, cp)


def _dilated_kernel(cur_ref, prev_ref, o_ref, lse_ref, *, dilation):
    first_key = jnp.where(pl.program_id(1) == 0, BAND, 0)
    qi = lax.broadcasted_iota(jnp.int32, (BAND, 2 * BAND), 0)
    ki = lax.broadcasted_iota(jnp.int32, (BAND, 2 * BAND), 1)
    band = (ki >= qi) & (ki <= qi + BAND)
    blocks = DIL_TILE // BAND
    scores, values = [], []
    for sb in range(blocks):
        rows = slice(sb * BAND, (sb + 1) * BAND)
        before = prev_ref[0, 0] if sb == 0 else cur_ref[0, 0, (sb - 1) * BAND:sb * BAND, :]
        mask = band & (ki >= first_key) if sb == 0 else band
        kv = jnp.concatenate([before, cur_ref[0, 0, rows, :]], axis=0)
        for h in range(HEADS_PER_GROUP):
            q = cur_ref[0, 0, rows, h * HEAD_DIM:(h + 1) * HEAD_DIM]
            k = kv[:, GROUP_W + h * HEAD_DIM:GROUP_W + (h + 1) * HEAD_DIM]
            scores.append(jnp.where(mask, _dot_nt(q, k), NEG))
            values.append(kv[:, 2 * GROUP_W + h * HEAD_DIM:2 * GROUP_W + (h + 1) * HEAD_DIM])
    outs, lses = [], []
    for s, v in zip(scores, values):
        m = jnp.max(s, axis=-1, keepdims=True)
        p = jnp.exp(s - m)
        l = jnp.sum(p, axis=-1, keepdims=True)
        outs.append(_dot(p.astype(BF16), v) / l)
        lses.append(jnp.broadcast_to(m + jnp.log(l), (BAND, HEAD_DIM)))
    per = HEADS_PER_GROUP
    o = jnp.concatenate([jnp.concatenate(outs[sb * per:(sb + 1) * per], axis=-1) for sb in range(blocks)], axis=0)
    lse = jnp.concatenate([jnp.concatenate(lses[sb * per:(sb + 1) * per], axis=-1) for sb in range(blocks)], axis=0)
    j = pl.program_id(2)
    for c in range(GROUP_W // LANES):
        lanes = slice(c * LANES, (c + 1) * LANES)
        if dilation == 1:
            o_ref[c, 0] = o[:, lanes]
            lse_ref[c, 0] = lse[:, lanes]
        else:
            o_ref[c, 0, pl.ds(j, DIL_TILE, stride=dilation), :] = o[:, lanes]
            lse_ref[c, 0, pl.ds(j, DIL_TILE, stride=dilation), :] = lse[:, lanes]


def _dilated_group(qkv, batch, seq, dilation):
    length = seq // dilation
    assert length % DIL_TILE == 0
    per = DIL_TILE // BAND
    halves = GROUP_W // LANES
    out_block = pl.BlockSpec((halves, 1, dilation * DIL_TILE, LANES), lambda b, t, j: (0, b, t, 0))
    return pl.pallas_call(
        functools.partial(_dilated_kernel, dilation=dilation),
        grid=(batch, length // DIL_TILE, dilation),
        in_specs=[pl.BlockSpec((1, 1, DIL_TILE, 3 * GROUP_W), lambda b, t, j: (b, j, t, 0)),
                  pl.BlockSpec((1, 1, BAND, 3 * GROUP_W), lambda b, t, j: (b, j, jnp.maximum(t * per - 1, 0), 0))],
        out_specs=[out_block, out_block],
        out_shape=[jax.ShapeDtypeStruct((halves, batch, seq, LANES), F32)] * 2,
        compiler_params=_cparams("parallel", "parallel", "arbitrary"),
        name=f"dilated_attn_d{dilation}",
    )(qkv, qkv)


def _out_a_kernel(x_ref, o0_ref, o1_ref, o2_ref, l0_ref, l1_ref, l2_ref, qm_ref, kv_ref, w_ref, g_ref, b_ref,
                  y_ref):
    scores = _memory_scores(qm_ref[...], kv_ref)
    halves = []
    for c in range(GROUP_W // LANES):
        l0, l1, l2 = l0_ref[c], l1_ref[c], l2_ref[c]
        m = jnp.maximum(jnp.maximum(l0, l1), l2)
        e0, e1, e2 = jnp.exp(l0 - m), jnp.exp(l1 - m), jnp.exp(l2 - m)
        halves.append((e0 * o0_ref[c] + e1 * o1_ref[c] + e2 * o2_ref[c]) / (e0 + e1 + e2))
    o_a = jnp.concatenate(halves, axis=1)
    mix = _dot(o_a.astype(BF16), w_ref[:GROUP_W, :])
    o_m = _memory_values(scores, kv_ref)
    mix = mix + _dot(o_m.astype(BF16), w_ref[GROUP_W:, :])
    y_ref[...] = _layer_norm(DEEPNORM_ALPHA * x_ref[...] + mix, g_ref[...], b_ref[...])


def _out_a(x, os, lses, qm, kv, w, g, b, seq):
    n = x.shape[0]
    per_seq = seq // ROW_TILE
    row = lambda w_: pl.BlockSpec((ROW_TILE, w_), lambda i: (i, 0))
    return pl.pallas_call(
        _out_a_kernel,
        grid=(n // ROW_TILE,),
        in_specs=[row(D_MODEL)] + [pl.BlockSpec((GROUP_W // LANES, ROW_TILE, LANES), lambda i: (0, i, 0))] * 6
                 + [row(MEM_W),
                  pl.BlockSpec((1, kv.shape[1], 2 * MEM_W), lambda i: (i // per_seq, 0, 0)),
                  _resident((GROUP_W + MEM_W, D_MODEL)), _resident((1, D_MODEL)), _resident((1, D_MODEL))],
        out_specs=row(D_MODEL),
        out_shape=jax.ShapeDtypeStruct((n, D_MODEL), F32),
        compiler_params=_cparams("parallel"),
        name="out_a_ln",
    )(x, *os, *lses, qm, kv, w, g, b)


def _decay_kernel(x_ref, wf_ref, fb_ref, ct_ref, cs_ref, carry_ref):
    @pl.when(pl.program_id(1) == 0)
    def _():
        carry_ref[...] = jnp.zeros_like(carry_ref)

    f = _dot(x_ref[...].astype(BF16), wf_ref[...])
    ft = f.T[:FOX_AUG] + fb_ref[...]
    logf = jnp.minimum(ft, 0.0) - jnp.log1p(jnp.exp(-jnp.abs(ft)))
    lane = lax.broadcasted_iota(jnp.int32, logf.shape, 1)
    c = logf
    shift = 1
    while shift < ROW_TILE:
        c = c + jnp.where(lane >= shift, pltpu.roll(c, shift, 1), 0.0)
        shift *= 2
    c = c + carry_ref[...]
    carry_ref[...] = c[:, ROW_TILE - 1:ROW_TILE]
    hi = c.astype(BF16)
    r1 = c - hi.astype(F32)
    mid = r1.astype(BF16)
    lo = (r1 - mid.astype(F32)).astype(BF16)
    ones = jnp.ones((FOX_AUG, ROW_TILE), BF16)
    zeros = jnp.zeros((LANES - 4 * FOX_AUG, ROW_TILE), BF16)
    stack = jnp.concatenate([hi, mid, lo, ones, zeros], axis=0)
    ct_ref[0] = stack
    cs_ref[0] = stack.astype(F32).T.astype(BF16)


def _decay(x, wf, fb, batch, seq):
    per_seq = seq // ROW_TILE
    return pl.pallas_call(
        _decay_kernel,
        grid=(batch, per_seq),
        in_specs=[pl.BlockSpec((ROW_TILE, D_MODEL), lambda b, i: (b * per_seq + i, 0)),
                  _resident((D_MODEL, LANES)), _resident((FOX_AUG, 1))],
        out_specs=[pl.BlockSpec((1, LANES, ROW_TILE), lambda b, i: (b, 0, i)),
                   pl.BlockSpec((1, ROW_TILE, LANES), lambda b, i: (b, i, 0))],
        out_shape=[jax.ShapeDtypeStruct((batch, LANES, seq), BF16), jax.ShapeDtypeStruct((batch, seq, LANES), BF16)],
        scratch_shapes=[pltpu.VMEM((FOX_AUG, 1), F32)],
        compiler_params=_cparams("parallel", "arbitrary"),
        name="decay_cumsum",
    )(x, wf, fb)


def _proj_b_kernel(x_ref, ct_ref, cs_ref, wqt_ref, wk_ref, wvt_ref, wqm_ref, pq_ref, pk_ref,
                   qt_ref, k_ref, vt_ref, qm_ref):
    xb = x_ref[...].astype(BF16)
    qt = _dot_nt(wqt_ref[...], xb)
    aug = _dot(pq_ref[...], ct_ref[0])
    vt = _dot_nt(wvt_ref[...], xb)
    kk = _dot(xb, wk_ref[...]) + _dot(cs_ref[0], pk_ref[...])
    row = lax.broadcasted_iota(jnp.int32, (FOX_AUG, ROW_TILE), 0)
    ones_row = jnp.where(row == 0, 1.0, 0.0).astype(BF16)
    for h in range(N_MIX_HEADS):
        hs = slice(h * HEAD_DIM, (h + 1) * HEAD_DIM)
        qt_ref[0, h, :HEAD_DIM, :] = qt[hs].astype(BF16)
        qt_ref[0, h, HEAD_DIM:, :] = aug[hs].astype(BF16)
        vt_ref[0, h, :HEAD_DIM, :] = vt[hs].astype(BF16)
        vt_ref[0, h, HEAD_DIM:, :] = ones_row
        k_ref[0, h] = kk[:, h * LANES:(h + 1) * LANES].astype(BF16)
    qm_ref[...] = _dot(xb, wqm_ref[...]).astype(BF16)


def _proj_b(x, ct, cs, wqt, wk, wvt, wqm, pq, pk, batch, seq):
    per_seq = seq // ROW_TILE
    v_rows = HEAD_DIM + FOX_AUG
    return pl.pallas_call(
        _proj_b_kernel,
        grid=(batch, per_seq),
        in_specs=[pl.BlockSpec((ROW_TILE, D_MODEL), lambda b, i: (b * per_seq + i, 0)),
                  pl.BlockSpec((1, LANES, ROW_TILE), lambda b, i: (b, 0, i)),
                  pl.BlockSpec((1, ROW_TILE, LANES), lambda b, i: (b, i, 0)),
                  _resident(wqt.shape), _resident(wk.shape), _resident(wvt.shape), _resident(wqm.shape),
                  _resident(pq.shape), _resident(pk.shape)],
        out_specs=[pl.BlockSpec((1, N_MIX_HEADS, LANES, ROW_TILE), lambda b, i: (b, 0, 0, i)),
                   pl.BlockSpec((1, N_MIX_HEADS, ROW_TILE, LANES), lambda b, i: (b, 0, i, 0)),
                   pl.BlockSpec((1, N_MIX_HEADS, v_rows, ROW_TILE), lambda b, i: (b, 0, 0, i)),
                   pl.BlockSpec((ROW_TILE, MEM_W), lambda b, i: (b * per_seq + i, 0))],
        out_shape=[jax.ShapeDtypeStruct((batch, N_MIX_HEADS, LANES, seq), BF16),
                   jax.ShapeDtypeStruct((batch, N_MIX_HEADS, seq, LANES), BF16),
                   jax.ShapeDtypeStruct((batch, N_MIX_HEADS, v_rows, seq), BF16),
                   jax.ShapeDtypeStruct((batch * seq, MEM_W), BF16)],
        compiler_params=_cparams("parallel", "parallel"),
        name="proj_b",
    )(x, ct, cs, wqt, wk, wvt, wqm, pq, pk)


def _fox_kernel(qt_ref, k_ref, vt_ref, o_ref, s_ref, p_ref):
    seq = k_ref.shape[2]
    heads = k_ref.shape[1]
    key_pos = lax.broadcasted_iota(jnp.int32, (FOX_TILE, FOX_TILE), 0)
    qry_pos = lax.broadcasted_iota(jnp.int32, (FOX_TILE, FOX_TILE), 1)
    future = key_pos > qry_pos
    for qi in range(seq // FOX_TILE):
        cols = slice(qi * FOX_TILE, (qi + 1) * FOX_TILE)
        n_keys = (qi + 1) * FOX_TILE
        outs = []
        for h in range(heads):
            slot = (qi * heads + h) % 2
            q = qt_ref[0, h, :, cols]
            m = None
            for kc in range(qi + 1):
                rows = slice(kc * FOX_TILE, (kc + 1) * FOX_TILE)
                s = _dot(k_ref[0, h, rows, :], q)
                if kc == qi:
                    s = jnp.where(future, NEG, s)
                s_ref[slot, rows, :] = s
                mc = jnp.max(s, axis=0, keepdims=True)
                m = mc if m is None else jnp.maximum(m, mc)
            for kc in range(qi + 1):
                rows = slice(kc * FOX_TILE, (kc + 1) * FOX_TILE)
                p_ref[slot, rows, :] = jnp.exp(s_ref[slot, rows, :] - m).astype(BF16)
            acc = _dot(vt_ref[0, h, :, :n_keys], p_ref[slot, :n_keys, :])
            outs.append(acc[:HEAD_DIM] / acc[HEAD_DIM:HEAD_DIM + 1])
        o_ref[0, cols, :] = jnp.concatenate(outs, axis=0).T.astype(BF16)


def _fox(qt, kk, vt, batch, seq):
    pair = LANES // HEAD_DIM
    v_rows = HEAD_DIM + FOX_AUG
    return pl.pallas_call(
        _fox_kernel,
        grid=(batch, N_MIX_HEADS // pair),
        in_specs=[pl.BlockSpec((1, pair, LANES, seq), lambda b, h: (b, h, 0, 0)),
                  pl.BlockSpec((1, pair, seq, LANES), lambda b, h: (b, h, 0, 0)),
                  pl.BlockSpec((1, pair, v_rows, seq), lambda b, h: (b, h, 0, 0))],
        out_specs=pl.BlockSpec((1, seq, LANES), lambda b, h: (b, 0, h)),
        out_shape=jax.ShapeDtypeStruct((batch, seq, MIX_W), BF16),
        scratch_shapes=[pltpu.VMEM((2, seq, FOX_TILE), F32), pltpu.VMEM((2, seq, FOX_TILE), BF16)],
        compiler_params=_cparams("parallel", "parallel"),
        name="fox_attn",
    )(qt, kk, vt)


def _out_b_kernel(x_ref, o_ref, qm_ref, kv_ref, w_ref, g_ref, b_ref, y_ref):
    scores = _memory_scores(qm_ref[...], kv_ref)
    mix = _dot(o_ref[...], w_ref[:MIX_W, :])
    o_m = _memory_values(scores, kv_ref)
    mix = mix + _dot(o_m.astype(BF16), w_ref[MIX_W:, :])
    y_ref[...] = _layer_norm(DEEPNORM_ALPHA * x_ref[...] + mix, g_ref[...], b_ref[...])


def _out_b(x, o, qm, kv, w, g, b, seq):
    n = x.shape[0]
    per_seq = seq // ROW_TILE
    row = lambda w_: pl.BlockSpec((ROW_TILE, w_), lambda i: (i, 0))
    return pl.pallas_call(
        _out_b_kernel,
        grid=(n // ROW_TILE,),
        in_specs=[row(D_MODEL), row(MIX_W), row(MEM_W),
                  pl.BlockSpec((1, kv.shape[1], 2 * MEM_W), lambda i: (i // per_seq, 0, 0)),
                  _resident((MIX_W + MEM_W, D_MODEL)), _resident((1, D_MODEL)), _resident((1, D_MODEL))],
        out_specs=row(D_MODEL),
        out_shape=jax.ShapeDtypeStruct((n, D_MODEL), F32),
        compiler_params=_cparams("parallel"),
        name="out_b_ln",
    )(x, o, qm, kv, w, g, b)


def _rope_tables(seq):
    pos = jnp.arange(seq, dtype=F32)
    inv_freq = 1.0 / (ROPE_THETA ** (jnp.arange(ROT_HALF, dtype=F32) / ROT_HALF))
    ang = pos[:, None] * inv_freq[None, :]
    cos, sin = jnp.cos(ang), jnp.sin(ang)
    ones = jnp.ones((seq, HEAD_DIM - ROT_DIM), F32)
    zeros = jnp.zeros((seq, HEAD_DIM - ROT_DIM), F32)
    zero_half = jnp.zeros((seq, ROT_HALF), F32)
    head = lambda parts: jnp.concatenate(parts * (LANES // HEAD_DIM), axis=1)
    return head([cos, cos, ones]), head([-sin, zero_half, zeros]), head([zero_half, sin, zeros])


def _placement_matrices():
    h = jnp.arange(N_MIX_HEADS)
    pq = jnp.zeros((MIX_W, LANES), F32)
    pk = jnp.zeros((LANES, N_MIX_HEADS * LANES), F32)
    for part in range(3):
        pq = pq.at[h * HEAD_DIM + part, part * FOX_AUG + h].set(1.0)
        pq = pq.at[h * HEAD_DIM + 3 + part, 3 * FOX_AUG].set(1.0)
        pk = pk.at[3 * FOX_AUG, h * LANES + HEAD_DIM + part].set(1.0)
        pk = pk.at[part * FOX_AUG + h, h * LANES + HEAD_DIM + 3 + part].set(-1.0)
    return pq.astype(BF16), pk.astype(BF16)


def kernel(x, mem, ffn1_w_gate_up, ffn1_w_down, ffn2_w_gate_up, ffn2_w_down, ln_gain, ln_bias, mem_w_kv,
           a_w_in, a_w_out, b_w_in, b_forget_bias, b_w_out):
    batch, seq, _ = x.shape
    n = batch * seq
    assert seq % ROW_TILE == 0 and all(seq % (d * DIL_TILE) == 0 for _, d in DILATED_GROUPS)
    xf = x.reshape(n, D_MODEL)
    mem2d = mem.reshape(batch * mem.shape[1], D_MODEL)
    gain = lambda i, j: ln_gain[i, j].reshape(1, D_MODEL)
    bias = lambda i, j: ln_bias[i, j].reshape(1, D_MODEL)
    ffn = lambda v, wgu, wd, i, j: _ffn(v, wgu.astype(BF16), wd.astype(BF16), gain(i, j), bias(i, j))

    xf = ffn(xf, ffn1_w_gate_up[0], ffn1_w_down[0], 0, 0)
    wa = a_w_in[0]
    cols = []
    for g in range(len(DILATED_GROUPS)):
        cols += [wa[:, part * MIX_W + g * GROUP_W:part * MIX_W + (g + 1) * GROUP_W] * (ATTN_SCALE if part == 0 else 1.0)
                 for part in range(3)]
    cols.append(wa[:, 3 * MIX_W:] * ATTN_SCALE)
    w_in_a = jnp.concatenate(cols, axis=1).astype(BF16)
    ca, cm, cp = _rope_tables(seq)
    qkv0, qkv1, qkv2, qm = _proj_a(xf, w_in_a, ca, cm, cp, batch, seq)
    os, lses = [], []
    for qkv, (_, dilation) in zip((qkv0, qkv1, qkv2), DILATED_GROUPS):
        o, lse = _dilated_group(qkv, batch, seq, dilation)
        os.append(o.reshape(GROUP_W // LANES, n, LANES))
        lses.append(lse.reshape(GROUP_W // LANES, n, LANES))
    kv = _memkv(mem2d, mem_w_kv[0].astype(BF16)).reshape(batch, mem.shape[1], 2 * MEM_W)
    xf = _out_a(xf, os, lses, qm, kv, a_w_out[0].astype(BF16), gain(0, 1), bias(0, 1), seq)
    xf = ffn(xf, ffn2_w_gate_up[0], ffn2_w_down[0], 0, 2)

    xf = ffn(xf, ffn1_w_gate_up[1], ffn1_w_down[1], 1, 0)
    wb = b_w_in[0]
    wf = jnp.pad(wb[:, 3 * MIX_W:3 * MIX_W + N_MIX_HEADS], ((0, 0), (0, LANES - N_MIX_HEADS))).astype(BF16)
    fb = jnp.pad(b_forget_bias[0].astype(F32), (0, FOX_AUG - N_MIX_HEADS)).reshape(FOX_AUG, 1)
    ct, cs = _decay(xf, wf, fb, batch, seq)
    wqt = (wb[:, :MIX_W] * ATTN_SCALE).T.astype(BF16)
    wk = jnp.pad(wb[:, MIX_W:2 * MIX_W].reshape(D_MODEL, N_MIX_HEADS, HEAD_DIM),
                 ((0, 0), (0, 0), (0, LANES - HEAD_DIM))).reshape(D_MODEL, N_MIX_HEADS * LANES).astype(BF16)
    wvt = wb[:, 2 * MIX_W:3 * MIX_W].T.astype(BF16)
    wqm = (wb[:, 3 * MIX_W + N_MIX_HEADS:] * ATTN_SCALE).astype(BF16)
    pq, pk = _placement_matrices()
    qt, kk, vt, qm = _proj_b(xf, ct, cs, wqt, wk, wvt, wqm, pq, pk, batch, seq)
    o = _fox(qt, kk, vt, batch, seq).reshape(n, MIX_W)
    kv = _memkv(mem2d, mem_w_kv[1].astype(BF16)).reshape(batch, mem.shape[1], 2 * MEM_W)
    xf = _out_b(xf, o, qm, kv, b_w_out[0].astype(BF16), gain(1, 1), bias(1, 1), seq)
    xf = ffn(xf, ffn2_w_gate_up[1], ffn2_w_down[1], 1, 2)
    return xf.reshape(batch, seq, D_MODEL)
```

```python
import functools

import jax
import jax.numpy as jnp
from jax import lax
from jax.experimental import pallas as pl
from jax.experimental.pallas import tpu as pltpu

D_MODEL = 1024
DEPTH = 2
HEAD_DIM = 64
N_MIX_HEADS = 12
N_MEM_HEADS = 4
DILATED_GROUPS = ((128, 1), (512, 4), (2048, 16))
HEADS_PER_GROUP = N_MIX_HEADS // len(DILATED_GROUPS)
ROT_DIM = HEAD_DIM // 4
ROT_HALF = ROT_DIM // 2
ROPE_THETA = 500000.0
D_FF = 2816
BAND = 128
DEEPNORM_ALPHA = (2 * DEPTH) ** 0.25
LN_EPS = 1e-5
MIX_W = N_MIX_HEADS * HEAD_DIM
MEM_W = N_MEM_HEADS * HEAD_DIM
GROUP_W = HEADS_PER_GROUP * HEAD_DIM
ATTN_SCALE = HEAD_DIM ** -0.5

LANES = 128
VMEM_LIMIT = 56 * 1024 * 1024

ROW_TILE = 512
FF_CHUNK = 256
FOX_TILE = 512
FOX_AUG = 16
DIL_TILE = 256

F32 = jnp.float32
BF16 = jnp.bfloat16
NEG = -1e30
LOG2_E = 1.4426950408889634


def _cparams(*sem):
    return pltpu.CompilerParams(dimension_semantics=sem, vmem_limit_bytes=VMEM_LIMIT)


def _resident(shape):
    return pl.BlockSpec(shape, lambda *_: (0,) * len(shape), pipeline_mode=pl.Buffered(1))


def _dot(a, b):
    return jnp.dot(a, b, preferred_element_type=F32)


def _dot_nt(a, b):
    return lax.dot_general(a, b, (((1,), (1,)), ((), ())), preferred_element_type=F32)


def _layer_norm(z, g, b):
    mu = jnp.mean(z, axis=-1, keepdims=True)
    zc = z - mu
    var = jnp.mean(zc * zc, axis=-1, keepdims=True)
    return zc * lax.rsqrt(var + LN_EPS) * g + b


def _ffn_kernel(x_ref, wgu_ref, wd_ref, g_ref, b_ref, o_ref, h_ref):
    half = ROW_TILE // 2
    for part in range(2):
        rows = slice(part * half, (part + 1) * half)
        x = x_ref[rows, :]
        xb = x.astype(BF16)
        for c in range(D_FF // FF_CHUNK):
            lo = c * FF_CHUNK
            gate = _dot(xb, wgu_ref[:, lo:lo + FF_CHUNK])
            up = _dot(xb, wgu_ref[:, D_FF + lo:D_FF + lo + FF_CHUNK])
            h_ref[rows, lo:lo + FF_CHUNK] = (gate * jax.nn.sigmoid(gate) * up).astype(BF16)
        y = _dot(h_ref[rows, :], wd_ref[...])
        o_ref[rows, :] = _layer_norm(DEEPNORM_ALPHA * x + 0.5 * y, g_ref[...], b_ref[...])


def _ffn(x, wgu, wd, g, b):
    n = x.shape[0]
    row = pl.BlockSpec((ROW_TILE, D_MODEL), lambda i: (i, 0))
    return pl.pallas_call(
        _ffn_kernel,
        grid=(n // ROW_TILE,),
        in_specs=[row, _resident((D_MODEL, 2 * D_FF)), _resident((D_FF, D_MODEL)),
                  _resident((1, D_MODEL)), _resident((1, D_MODEL))],
        out_specs=row,
        out_shape=jax.ShapeDtypeStruct((n, D_MODEL), F32),
        scratch_shapes=[pltpu.VMEM((ROW_TILE, D_FF), BF16)],
        compiler_params=_cparams("parallel"),
        name="ffn_ln",
    )(x, wgu, wd, g, b)


def _memkv_kernel(m_ref, w_ref, o_ref):
    o_ref[...] = _dot(m_ref[...].astype(BF16), w_ref[...]).astype(BF16)


def _memkv(mem2d, w):
    n = mem2d.shape[0]
    tile = min(ROW_TILE, n)
    return pl.pallas_call(
        _memkv_kernel,
        grid=(n // tile,),
        in_specs=[pl.BlockSpec((tile, D_MODEL), lambda i: (i, 0)), _resident((D_MODEL, 2 * MEM_W))],
        out_specs=pl.BlockSpec((tile, 2 * MEM_W), lambda i: (i, 0)),
        out_shape=jax.ShapeDtypeStruct((n, 2 * MEM_W), BF16),
        compiler_params=_cparams("parallel"),
        name="mem_kv",
    )(mem2d, w)


def _memory_scores(qm, kv_ref):
    return [_dot_nt(qm[:, h * HEAD_DIM:(h + 1) * HEAD_DIM], kv_ref[0, :, h * HEAD_DIM:(h + 1) * HEAD_DIM])
            for h in range(N_MEM_HEADS)]


def _memory_values(scores, kv_ref):
    outs = []
    for h, s in enumerate(scores):
        vm = kv_ref[0, :, MEM_W + h * HEAD_DIM:MEM_W + (h + 1) * HEAD_DIM]
        p = jnp.exp(s - jnp.max(s, axis=-1, keepdims=True))
        l = jnp.sum(p, axis=-1, keepdims=True)
        outs.append(_dot(p.astype(BF16), vm) / l)
    return jnp.concatenate(outs, axis=-1)


def _proj_a_kernel(x_ref, w_ref, ca_ref, cm_ref, cp_ref, o0_ref, o1_ref, o2_ref, qm_ref, stage_ref):
    xb = x_ref[...].astype(BF16)
    ca, cm, cp = ca_ref[...], cm_ref[...], cp_ref[...]
    rot_blocks = 2 * GROUP_W // LANES
    for g, o_ref in enumerate((o0_ref, o1_ref, o2_ref)):
        dilation = DILATED_GROUPS[g][1]
        per = ROW_TILE // dilation
        h = _dot(xb, w_ref[:, g * 3 * GROUP_W:(g + 1) * 3 * GROUP_W])
        for cb in range(3 * GROUP_W // LANES):
            t = h[:, cb * LANES:(cb + 1) * LANES]
            if cb < rot_blocks:
                t = t * ca + pltpu.roll(t, LANES - ROT_HALF, 1) * cm + pltpu.roll(t, ROT_HALF, 1) * cp
            if dilation == 1:
                o_ref[0, 0, :, cb * LANES:(cb + 1) * LANES] = t.astype(BF16)
            else:
                stage_ref[cb] = t
        if dilation > 1:
            for j in range(dilation):
                rows = [stage_ref[cb, pl.ds(j, per, stride=dilation), :] for cb in range(3 * GROUP_W // LANES)]
                o_ref[0, j] = jnp.concatenate(rows, axis=1).astype(BF16)
    qm_ref[...] = _dot(xb, w_ref[:, 3 * MIX_W:]).astype(BF16)


def _proj_a(x, w, ca, cm, cp, batch, seq):
    per_seq = seq // ROW_TILE
    tab = pl.BlockSpec((ROW_TILE, LANES), lambda b, i: (i, 0))
    grouped = lambda d: pl.BlockSpec((1, d, ROW_TILE // d, 3 * GROUP_W), lambda b, i: (b, 0, i, 0))
    return pl.pallas_call(
        _proj_a_kernel,
        grid=(batch, per_seq),
        in_specs=[pl.BlockSpec((ROW_TILE, D_MODEL), lambda b, i: (b * per_seq + i, 0)),
                  _resident((D_MODEL, 3 * MIX_W + MEM_W)), tab, tab, tab],
        out_specs=[grouped(d) for _, d in DILATED_GROUPS]
                  + [pl.BlockSpec((ROW_TILE, MEM_W), lambda b, i: (b * per_seq + i, 0))],
        out_shape=[jax.ShapeDtypeStruct((batch, d, seq // d, 3 * GROUP_W), BF16) for _, d in DILATED_GROUPS]
                  + [jax.ShapeDtypeStruct((batch * seq, MEM_W), BF16)],
        scratch_shapes=[pltpu.VMEM((3 * GROUP_W // LANES, ROW_TILE, LANES), F32)],
        compiler_params=_cparams("parallel", "parallel"),
        name="proj_a",
    )(x, w, ca, cm, cp)


def _dilated_kernel(cur_ref, prev_ref, o_ref, lse_ref, *, dilation):
    first_key = jnp.where(pl.program_id(1) == 0, BAND, 0)
    qi = lax.broadcasted_iota(jnp.int32, (BAND, 2 * BAND), 0)
    ki = lax.broadcasted_iota(jnp.int32, (BAND, 2 * BAND), 1)
    band = (ki >= qi) & (ki <= qi + BAND)
    blocks = DIL_TILE // BAND
    scores, values = [], []
    for sb in range(blocks):
        rows = slice(sb * BAND, (sb + 1) * BAND)
        before = prev_ref[0, 0] if sb == 0 else cur_ref[0, 0, (sb - 1) * BAND:sb * BAND, :]
        mask = band & (ki >= first_key) if sb == 0 else band
        kv = jnp.concatenate([before, cur_ref[0, 0, rows, :]], axis=0)
        for h in range(HEADS_PER_GROUP):
            q = cur_ref[0, 0, rows, h * HEAD_DIM:(h + 1) * HEAD_DIM]
            k = kv[:, GROUP_W + h * HEAD_DIM:GROUP_W + (h + 1) * HEAD_DIM]
            scores.append(jnp.where(mask, _dot_nt(q, k), NEG))
            values.append(kv[:, 2 * GROUP_W + h * HEAD_DIM:2 * GROUP_W + (h + 1) * HEAD_DIM])
    outs, lses = [], []
    for s, v in zip(scores, values):
        m = jnp.max(s, axis=-1, keepdims=True)
        p = jnp.exp(s - m)
        l = jnp.sum(p, axis=-1, keepdims=True)
        outs.append(_dot(p.astype(BF16), v) / l)
        lses.append(jnp.broadcast_to(m + jnp.log(l), (BAND, HEAD_DIM)))
    per = HEADS_PER_GROUP
    o = jnp.concatenate([jnp.concatenate(outs[sb * per:(sb + 1) * per], axis=-1) for sb in range(blocks)], axis=0)
    lse = jnp.concatenate([jnp.concatenate(lses[sb * per:(sb + 1) * per], axis=-1) for sb in range(blocks)], axis=0)
    j = pl.program_id(2)
    for c in range(GROUP_W // LANES):
        lanes = slice(c * LANES, (c + 1) * LANES)
        if dilation == 1:
            o_ref[c, 0] = o[:, lanes]
            lse_ref[c, 0] = lse[:, lanes]
        else:
            o_ref[c, 0, pl.ds(j, DIL_TILE, stride=dilation), :] = o[:, lanes]
            lse_ref[c, 0, pl.ds(j, DIL_TILE, stride=dilation), :] = lse[:, lanes]


def _dilated_group(qkv, batch, seq, dilation):
    length = seq // dilation
    assert length % DIL_TILE == 0
    per = DIL_TILE // BAND
    halves = GROUP_W // LANES
    out_block = pl.BlockSpec((halves, 1, dilation * DIL_TILE, LANES), lambda b, t, j: (0, b, t, 0))
    return pl.pallas_call(
        functools.partial(_dilated_kernel, dilation=dilation),
        grid=(batch, length // DIL_TILE, dilation),
        in_specs=[pl.BlockSpec((1, 1, DIL_TILE, 3 * GROUP_W), lambda b, t, j: (b, j, t, 0)),
                  pl.BlockSpec((1, 1, BAND, 3 * GROUP_W), lambda b, t, j: (b, j, jnp.maximum(t * per - 1, 0), 0))],
        out_specs=[out_block, out_block],
        out_shape=[jax.ShapeDtypeStruct((halves, batch, seq, LANES), F32)] * 2,
        compiler_params=_cparams("parallel", "parallel", "arbitrary"),
        name=f"dilated_attn_d{dilation}",
    )(qkv, qkv)


def _out_a_kernel(x_ref, o0_ref, o1_ref, o2_ref, l0_ref, l1_ref, l2_ref, qm_ref, kv_ref, w_ref, g_ref, b_ref,
                  y_ref):
    scores = _memory_scores(qm_ref[...], kv_ref)
    halves = []
    for c in range(GROUP_W // LANES):
        l0, l1, l2 = l0_ref[c], l1_ref[c], l2_ref[c]
        m = jnp.maximum(jnp.maximum(l0, l1), l2)
        e0, e1, e2 = jnp.exp(l0 - m), jnp.exp(l1 - m), jnp.exp(l2 - m)
        halves.append((e0 * o0_ref[c] + e1 * o1_ref[c] + e2 * o2_ref[c]) / (e0 + e1 + e2))
    o_a = jnp.concatenate(halves, axis=1)
    mix = _dot(o_a.astype(BF16), w_ref[:GROUP_W, :])
    o_m = _memory_values(scores, kv_ref)
    mix = mix + _dot(o_m.astype(BF16), w_ref[GROUP_W:, :])
    y_ref[...] = _layer_norm(DEEPNORM_ALPHA * x_ref[...] + mix, g_ref[...], b_ref[...])


def _out_a(x, os, lses, qm, kv, w, g, b, seq):
    n = x.shape[0]
    per_seq = seq // ROW_TILE
    row = lambda w_: pl.BlockSpec((ROW_TILE, w_), lambda i: (i, 0))
    return pl.pallas_call(
        _out_a_kernel,
        grid=(n // ROW_TILE,),
        in_specs=[row(D_MODEL)] + [pl.BlockSpec((GROUP_W // LANES, ROW_TILE, LANES), lambda i: (0, i, 0))] * 6
                 + [row(MEM_W),
                  pl.BlockSpec((1, kv.shape[1], 2 * MEM_W), lambda i: (i // per_seq, 0, 0)),
                  _resident((GROUP_W + MEM_W, D_MODEL)), _resident((1, D_MODEL)), _resident((1, D_MODEL))],
        out_specs=row(D_MODEL),
        out_shape=jax.ShapeDtypeStruct((n, D_MODEL), F32),
        compiler_params=_cparams("parallel"),
        name="out_a_ln",
    )(x, *os, *lses, qm, kv, w, g, b)


def _decay_kernel(x_ref, wf_ref, fb_ref, ct_ref, cs_ref, carry_ref):
    @pl.when(pl.program_id(1) == 0)
    def _():
        carry_ref[...] = jnp.zeros_like(carry_ref)

    f = _dot(x_ref[...].astype(BF16), wf_ref[...])
    ft = f.T[:FOX_AUG] + fb_ref[...]
    logf = jnp.minimum(ft, 0.0) - jnp.log1p(jnp.exp(-jnp.abs(ft)))
    lane = lax.broadcasted_iota(jnp.int32, logf.shape, 1)
    c = logf
    shift = 1
    while shift < ROW_TILE:
        c = c + jnp.where(lane >= shift, pltpu.roll(c, shift, 1), 0.0)
        shift *= 2
    c = c + carry_ref[...]
    carry_ref[...] = c[:, ROW_TILE - 1:ROW_TILE]
    c = c * LOG2_E
    hi = c.astype(BF16)
    r1 = c - hi.astype(F32)
    mid = r1.astype(BF16)
    lo = (r1 - mid.astype(F32)).astype(BF16)
    ones = jnp.ones((FOX_AUG, ROW_TILE), BF16)
    zeros = jnp.zeros((LANES - 4 * FOX_AUG, ROW_TILE), BF16)
    stack = jnp.concatenate([hi, mid, lo, ones, zeros], axis=0)
    ct_ref[0] = stack
    cs_ref[0] = stack.astype(F32).T.astype(BF16)


def _decay(x, wf, fb, batch, seq):
    per_seq = seq // ROW_TILE
    return pl.pallas_call(
        _decay_kernel,
        grid=(batch, per_seq),
        in_specs=[pl.BlockSpec((ROW_TILE, D_MODEL), lambda b, i: (b * per_seq + i, 0)),
                  _resident((D_MODEL, LANES)), _resident((FOX_AUG, 1))],
        out_specs=[pl.BlockSpec((1, LANES, ROW_TILE), lambda b, i: (b, 0, i)),
                   pl.BlockSpec((1, ROW_TILE, LANES), lambda b, i: (b, i, 0))],
        out_shape=[jax.ShapeDtypeStruct((batch, LANES, seq), BF16), jax.ShapeDtypeStruct((batch, seq, LANES), BF16)],
        scratch_shapes=[pltpu.VMEM((FOX_AUG, 1), F32)],
        compiler_params=_cparams("parallel", "arbitrary"),
        name="decay_cumsum",
    )(x, wf, fb)


def _proj_b_kernel(x_ref, ct_ref, cs_ref, wqt_ref, wk_ref, wvt_ref, wqm_ref, pq_ref, pk_ref,
                   qt_ref, k_ref, vt_ref, qm_ref):
    xb = x_ref[...].astype(BF16)
    qt = _dot_nt(wqt_ref[...], xb)
    aug = _dot(pq_ref[...], ct_ref[0])
    vt = _dot_nt(wvt_ref[...], xb)
    kk = _dot(xb, wk_ref[...]) + _dot(cs_ref[0], pk_ref[...])
    row = lax.broadcasted_iota(jnp.int32, (FOX_AUG, ROW_TILE), 0)
    ones_row = jnp.where(row == 0, 1.0, 0.0).astype(BF16)
    for h in range(N_MIX_HEADS):
        hs = slice(h * HEAD_DIM, (h + 1) * HEAD_DIM)
        qt_ref[0, h, :HEAD_DIM, :] = qt[hs].astype(BF16)
        qt_ref[0, h, HEAD_DIM:, :] = aug[hs].astype(BF16)
        vt_ref[0, h, :HEAD_DIM, :] = vt[hs].astype(BF16)
        vt_ref[0, h, HEAD_DIM:, :] = ones_row
        k_ref[0, h] = kk[:, h * LANES:(h + 1) * LANES].astype(BF16)
    qm_ref[...] = _dot(xb, wqm_ref[...]).astype(BF16)


def _proj_b(x, ct, cs, wqt, wk, wvt, wqm, pq, pk, batch, seq):
    per_seq = seq // ROW_TILE
    v_rows = HEAD_DIM + FOX_AUG
    return pl.pallas_call(
        _proj_b_kernel,
        grid=(batch, per_seq),
        in_specs=[pl.BlockSpec((ROW_TILE, D_MODEL), lambda b, i: (b * per_seq + i, 0)),
                  pl.BlockSpec((1, LANES, ROW_TILE), lambda b, i: (b, 0, i)),
                  pl.BlockSpec((1, ROW_TILE, LANES), lambda b, i: (b, i, 0)),
                  _resident(wqt.shape), _resident(wk.shape), _resident(wvt.shape), _resident(wqm.shape),
                  _resident(pq.shape), _resident(pk.shape)],
        out_specs=[pl.BlockSpec((1, N_MIX_HEADS, LANES, ROW_TILE), lambda b, i: (b, 0, 0, i)),
                   pl.BlockSpec((1, N_MIX_HEADS, ROW_TILE, LANES), lambda b, i: (b, 0, i, 0)),
                   pl.BlockSpec((1, N_MIX_HEADS, v_rows, ROW_TILE), lambda b, i: (b, 0, 0, i)),
                   pl.BlockSpec((ROW_TILE, MEM_W), lambda b, i: (b * per_seq + i, 0))],
        out_shape=[jax.ShapeDtypeStruct((batch, N_MIX_HEADS, LANES, seq), BF16),
                   jax.ShapeDtypeStruct((batch, N_MIX_HEADS, seq, LANES), BF16),
                   jax.ShapeDtypeStruct((batch, N_MIX_HEADS, v_rows, seq), BF16),
                   jax.ShapeDtypeStruct((batch * seq, MEM_W), BF16)],
        compiler_params=_cparams("parallel", "parallel"),
        name="proj_b",
    )(x, ct, cs, wqt, wk, wvt, wqm, pq, pk)


def _fox_kernel(qt_ref, k_ref, vt_ref, o_ref, s_ref, p_ref):
    seq = k_ref.shape[2]
    heads = k_ref.shape[1]
    key_pos = lax.broadcasted_iota(jnp.int32, (FOX_TILE, FOX_TILE), 0)
    qry_pos = lax.broadcasted_iota(jnp.int32, (FOX_TILE, FOX_TILE), 1)
    future = key_pos > qry_pos
    items = [(qi, h) for qi in range(seq // FOX_TILE) for h in range(heads)]

    def scores(i):
        qi, h = items[i]
        q = qt_ref[0, h, :, qi * FOX_TILE:(qi + 1) * FOX_TILE]
        m = None
        for kc in range(qi + 1):
            rows = slice(kc * FOX_TILE, (kc + 1) * FOX_TILE)
            s = _dot(k_ref[0, h, rows, :], q)
            if kc == qi:
                s = jnp.where(future, NEG, s)
            s_ref[i % 2, rows, :] = s
            mc = jnp.max(s, axis=0, keepdims=True)
            m = mc if m is None else jnp.maximum(m, mc)
        return m

    m_next = scores(0)
    outs = []
    for i, (qi, h) in enumerate(items):
        m = m_next
        if i + 1 < len(items):
            m_next = scores(i + 1)
        n_keys = (qi + 1) * FOX_TILE
        for kc in range(qi + 1):
            rows = slice(kc * FOX_TILE, (kc + 1) * FOX_TILE)
            p_ref[i % 2, rows, :] = jnp.exp2(s_ref[i % 2, rows, :] - m).astype(BF16)
        acc = _dot(vt_ref[0, h, :, :n_keys], p_ref[i % 2, :n_keys, :])
        outs.append(acc[:HEAD_DIM] / acc[HEAD_DIM:HEAD_DIM + 1])
        if h == heads - 1:
            o_ref[0, qi * FOX_TILE:(qi + 1) * FOX_TILE, :] = jnp.concatenate(outs, axis=0).T.astype(BF16)
            outs = []


def _fox(qt, kk, vt, batch, seq):
    pair = LANES // HEAD_DIM
    v_rows = HEAD_DIM + FOX_AUG
    return pl.pallas_call(
        _fox_kernel,
        grid=(batch, N_MIX_HEADS // pair),
        in_specs=[pl.BlockSpec((1, pair, LANES, seq), lambda b, h: (b, h, 0, 0)),
                  pl.BlockSpec((1, pair, seq, LANES), lambda b, h: (b, h, 0, 0)),
                  pl.BlockSpec((1, pair, v_rows, seq), lambda b, h: (b, h, 0, 0))],
        out_specs=pl.BlockSpec((1, seq, LANES), lambda b, h: (b, 0, h)),
        out_shape=jax.ShapeDtypeStruct((batch, seq, MIX_W), BF16),
        scratch_shapes=[pltpu.VMEM((2, seq, FOX_TILE), F32), pltpu.VMEM((2, seq, FOX_TILE), BF16)],
        compiler_params=_cparams("parallel", "parallel"),
        name="fox_attn",
    )(qt, kk, vt)


def _out_b_kernel(x_ref, o_ref, qm_ref, kv_ref, w_ref, g_ref, b_ref, y_ref):
    scores = _memory_scores(qm_ref[...], kv_ref)
    mix = _dot(o_ref[...], w_ref[:MIX_W, :])
    o_m = _memory_values(scores, kv_ref)
    mix = mix + _dot(o_m.astype(BF16), w_ref[MIX_W:, :])
    y_ref[...] = _layer_norm(DEEPNORM_ALPHA * x_ref[...] + mix, g_ref[...], b_ref[...])


def _out_b(x, o, qm, kv, w, g, b, seq):
    n = x.shape[0]
    per_seq = seq // ROW_TILE
    row = lambda w_: pl.BlockSpec((ROW_TILE, w_), lambda i: (i, 0))
    return pl.pallas_call(
        _out_b_kernel,
        grid=(n // ROW_TILE,),
        in_specs=[row(D_MODEL), row(MIX_W), row(MEM_W),
                  pl.BlockSpec((1, kv.shape[1], 2 * MEM_W), lambda i: (i // per_seq, 0, 0)),
                  _resident((MIX_W + MEM_W, D_MODEL)), _resident((1, D_MODEL)), _resident((1, D_MODEL))],
        out_specs=row(D_MODEL),
        out_shape=jax.ShapeDtypeStruct((n, D_MODEL), F32),
        compiler_params=_cparams("parallel"),
        name="out_b_ln",
    )(x, o, qm, kv, w, g, b)


def _rope_tables(seq):
    pos = jnp.arange(seq, dtype=F32)
    inv_freq = 1.0 / (ROPE_THETA ** (jnp.arange(ROT_HALF, dtype=F32) / ROT_HALF))
    ang = pos[:, None] * inv_freq[None, :]
    cos, sin = jnp.cos(ang), jnp.sin(ang)
    ones = jnp.ones((seq, HEAD_DIM - ROT_DIM), F32)
    zeros = jnp.zeros((seq, HEAD_DIM - ROT_DIM), F32)
    zero_half = jnp.zeros((seq, ROT_HALF), F32)
    head = lambda parts: jnp.concatenate(parts * (LANES // HEAD_DIM), axis=1)
    return head([cos, cos, ones]), head([-sin, zero_half, zeros]), head([zero_half, sin, zeros])


def _placement_matrices():
    h = jnp.arange(N_MIX_HEADS)
    pq = jnp.zeros((MIX_W, LANES), F32)
    pk = jnp.zeros((LANES, N_MIX_HEADS * LANES), F32)
    for part in range(3):
        pq = pq.at[h * HEAD_DIM + part, part * FOX_AUG + h].set(1.0)
        pq = pq.at[h * HEAD_DIM + 3 + part, 3 * FOX_AUG].set(1.0)
        pk = pk.at[3 * FOX_AUG, h * LANES + HEAD_DIM + part].set(1.0)
        pk = pk.at[part * FOX_AUG + h, h * LANES + HEAD_DIM + 3 + part].set(-1.0)
    return pq.astype(BF16), pk.astype(BF16)


def _layer_a_mixer(xf, mem, w_in, w_mem_kv, w_out, g, b, batch, seq):
    n = batch * seq
    cols = []
    for grp in range(len(DILATED_GROUPS)):
        cols += [w_in[:, part * MIX_W + grp * GROUP_W:part * MIX_W + (grp + 1) * GROUP_W]
                 * (ATTN_SCALE if part == 0 else 1.0) for part in range(3)]
    cols.append(w_in[:, 3 * MIX_W:] * ATTN_SCALE)
    w_in_a = jnp.concatenate(cols, axis=1).astype(BF16)
    ca, cm, cp = _rope_tables(seq)
    qkv0, qkv1, qkv2, qm = _proj_a(xf, w_in_a, ca, cm, cp, batch, seq)
    os, lses = [], []
    for qkv, (_, dilation) in zip((qkv0, qkv1, qkv2), DILATED_GROUPS):
        o, lse = _dilated_group(qkv, batch, seq, dilation)
        os.append(o.reshape(GROUP_W // LANES, n, LANES))
        lses.append(lse.reshape(GROUP_W // LANES, n, LANES))
    kv = _memkv(mem.reshape(batch * mem.shape[1], D_MODEL), w_mem_kv.astype(BF16))
    kv = kv.reshape(batch, mem.shape[1], 2 * MEM_W)
    return _out_a(xf, os, lses, qm, kv, w_out.astype(BF16), g, b, seq)


def _layer_b_mixer(xf, mem, w_in, forget_bias, w_mem_kv, w_out, g, b, batch, seq):
    n = batch * seq
    wf = jnp.pad(w_in[:, 3 * MIX_W:3 * MIX_W + N_MIX_HEADS], ((0, 0), (0, LANES - N_MIX_HEADS))).astype(BF16)
    fb = jnp.pad(forget_bias.astype(F32), (0, FOX_AUG - N_MIX_HEADS)).reshape(FOX_AUG, 1)
    ct, cs = _decay(xf, wf, fb, batch, seq)
    wqt = (w_in[:, :MIX_W] * (ATTN_SCALE * LOG2_E)).T.astype(BF16)
    wk = jnp.pad(w_in[:, MIX_W:2 * MIX_W].reshape(D_MODEL, N_MIX_HEADS, HEAD_DIM),
                 ((0, 0), (0, 0), (0, LANES - HEAD_DIM))).reshape(D_MODEL, N_MIX_HEADS * LANES).astype(BF16)
    wvt = w_in[:, 2 * MIX_W:3 * MIX_W].T.astype(BF16)
    wqm = (w_in[:, 3 * MIX_W + N_MIX_HEADS:] * ATTN_SCALE).astype(BF16)
    pq, pk = _placement_matrices()
    qt, kk, vt, qm = _proj_b(xf, ct, cs, wqt, wk, wvt, wqm, pq, pk, batch, seq)
    o = _fox(qt, kk, vt, batch, seq).reshape(n, MIX_W)
    kv = _memkv(mem.reshape(batch * mem.shape[1], D_MODEL), w_mem_kv.astype(BF16))
    kv = kv.reshape(batch, mem.shape[1], 2 * MEM_W)
    return _out_b(xf, o, qm, kv, w_out.astype(BF16), g, b, seq)


def kernel(x, mem, ffn1_w_gate_up, ffn1_w_down, ffn2_w_gate_up, ffn2_w_down, ln_gain, ln_bias, mem_w_kv,
           a_w_in, a_w_out, b_w_in, b_forget_bias, b_w_out):
    batch, seq, _ = x.shape
    n = batch * seq
    assert seq % ROW_TILE == 0 and all(seq % (d * DIL_TILE) == 0 for _, d in DILATED_GROUPS)
    xf = x.reshape(n, D_MODEL)
    gain = lambda i, j: ln_gain[i, j].reshape(1, D_MODEL)
    bias = lambda i, j: ln_bias[i, j].reshape(1, D_MODEL)
    ffn = lambda v, wgu, wd, i, j: _ffn(v, wgu.astype(BF16), wd.astype(BF16), gain(i, j), bias(i, j))

    xf = ffn(xf, ffn1_w_gate_up[0], ffn1_w_down[0], 0, 0)
    xf = _layer_a_mixer(xf, mem, a_w_in[0], mem_w_kv[0], a_w_out[0], gain(0, 1), bias(0, 1), batch, seq)
    xf = ffn(xf, ffn2_w_gate_up[0], ffn2_w_down[0], 0, 2)

    xf = ffn(xf, ffn1_w_gate_up[1], ffn1_w_down[1], 1, 0)
    xf = _layer_b_mixer(xf, mem, b_w_in[0], b_forget_bias[0], mem_w_kv[1], b_w_out[0], gain(1, 1), bias(1, 1),
                        batch, seq)
    xf = ffn(xf, ffn2_w_gate_up[1], ffn2_w_down[1], 1, 2)
    return xf.reshape(batch, seq, D_MODEL)
```

```python
import functools

import jax
import jax.numpy as jnp
from jax import lax
from jax.experimental import pallas as pl
from jax.experimental.pallas import tpu as pltpu

D_MODEL = 1024
DEPTH = 2
HEAD_DIM = 64
N_MIX_HEADS = 12
N_MEM_HEADS = 4
DILATED_GROUPS = ((128, 1), (512, 4), (2048, 16))
HEADS_PER_GROUP = N_MIX_HEADS // len(DILATED_GROUPS)
ROT_DIM = HEAD_DIM // 4
ROT_HALF = ROT_DIM // 2
ROPE_THETA = 500000.0
D_FF = 2816
BAND = 128
DEEPNORM_ALPHA = (2 * DEPTH) ** 0.25
LN_EPS = 1e-5
MIX_W = N_MIX_HEADS * HEAD_DIM
MEM_W = N_MEM_HEADS * HEAD_DIM
GROUP_W = HEADS_PER_GROUP * HEAD_DIM
ATTN_SCALE = HEAD_DIM ** -0.5

LANES = 128
VMEM_LIMIT = 56 * 1024 * 1024

ROW_TILE = 1024
FFN_TILE = 512
FF_CHUNK = 256
FOX_TILE = 512
FOX_AUG = 16
FOX_SLOTS = 3
DIL_TILE = 1024

F32 = jnp.float32
BF16 = jnp.bfloat16
NEG = -1e30
LOG2_E = 1.4426950408889634


def _cparams(*sem):
    return pltpu.CompilerParams(dimension_semantics=sem, vmem_limit_bytes=VMEM_LIMIT)


def _resident(shape):
    return pl.BlockSpec(shape, lambda *_: (0,) * len(shape), pipeline_mode=pl.Buffered(1))


def _dot(a, b):
    return jnp.dot(a, b, preferred_element_type=F32)


def _dot_nt(a, b):
    return lax.dot_general(a, b, (((1,), (1,)), ((), ())), preferred_element_type=F32)


def _layer_norm(z, g, b):
    mu = jnp.mean(z, axis=-1, keepdims=True)
    zc = z - mu
    var = jnp.mean(zc * zc, axis=-1, keepdims=True)
    return zc * lax.rsqrt(var + LN_EPS) * g + b


def _ffn_kernel(x_ref, wgu_ref, wd_ref, g_ref, b_ref, o_ref, h_ref):
    x = x_ref[...]
    xb = x.astype(BF16)
    for c in range(D_FF // FF_CHUNK):
        lo = c * FF_CHUNK
        gate = _dot(xb, wgu_ref[0, :, lo:lo + FF_CHUNK].astype(BF16))
        up = _dot(xb, wgu_ref[0, :, D_FF + lo:D_FF + lo + FF_CHUNK].astype(BF16))
        h_ref[:, lo:lo + FF_CHUNK] = (gate * jax.nn.sigmoid(gate) * up).astype(BF16)
    y = _dot(h_ref[...], wd_ref[0].astype(BF16))
    o_ref[...] = _layer_norm(DEEPNORM_ALPHA * x + 0.5 * y, g_ref[...], b_ref[...])


def _ffn(x, wgu_all, wd_all, layer, g, b):
    n = x.shape[0]
    row = pl.BlockSpec((FFN_TILE, D_MODEL), lambda i: (i, 0))
    one_layer = lambda shape: pl.BlockSpec((1,) + shape, lambda i: (layer, 0, 0), pipeline_mode=pl.Buffered(1))
    return pl.pallas_call(
        _ffn_kernel,
        grid=(n // FFN_TILE,),
        in_specs=[row, one_layer((D_MODEL, 2 * D_FF)), one_layer((D_FF, D_MODEL)),
                  _resident((1, D_MODEL)), _resident((1, D_MODEL))],
        out_specs=row,
        out_shape=jax.ShapeDtypeStruct((n, D_MODEL), F32),
        scratch_shapes=[pltpu.VMEM((FFN_TILE, D_FF), BF16)],
        compiler_params=_cparams("parallel"),
        name="ffn_ln",
    )(x, wgu_all, wd_all, g, b)


def _memkv_kernel(m_ref, w_ref, o_ref):
    o_ref[...] = _dot(m_ref[...].astype(BF16), w_ref[...].astype(BF16)).astype(BF16)


def _memkv(mem2d, w):
    n = mem2d.shape[0]
    tile = min(ROW_TILE, n)
    return pl.pallas_call(
        _memkv_kernel,
        grid=(n // tile,),
        in_specs=[pl.BlockSpec((tile, D_MODEL), lambda i: (i, 0)), _resident((D_MODEL, 2 * MEM_W))],
        out_specs=pl.BlockSpec((tile, 2 * MEM_W), lambda i: (i, 0)),
        out_shape=jax.ShapeDtypeStruct((n, 2 * MEM_W), BF16),
        compiler_params=_cparams("parallel"),
        name="mem_kv",
    )(mem2d, w)


def _head_of_pair(block, odd):
    lane = lax.broadcasted_iota(jnp.int32, block.shape, 1)
    keep = lane >= HEAD_DIM if odd else lane < HEAD_DIM
    return jnp.where(keep, block, jnp.zeros_like(block))


def _pair_softmax_values(s_even, s_odd, v_pair):
    ones = jnp.ones((v_pair.shape[0], LANES), BF16)
    res, stats = [], []
    for s in (s_even, s_odd):
        m = jnp.max(s, axis=-1, keepdims=True)
        p = jnp.exp(s - m).astype(BF16)
        l = _dot(p, ones)
        res.append(_dot(p, v_pair) / l)
        stats.append((m, l))
    lane = lax.broadcasted_iota(jnp.int32, res[0].shape, 1)
    return jnp.where(lane < HEAD_DIM, res[0], res[1]), stats


def _memory_scores(qm, kv_ref):
    scores = []
    for pair in range(MEM_W // LANES):
        lanes = slice(pair * LANES, (pair + 1) * LANES)
        km = kv_ref[0, :, lanes]
        scores += [_dot_nt(_head_of_pair(qm[:, lanes], odd), km) for odd in (False, True)]
    return scores


def _memory_values(scores, kv_ref):
    outs = []
    for pair in range(MEM_W // LANES):
        vm = kv_ref[0, :, MEM_W + pair * LANES:MEM_W + (pair + 1) * LANES]
        outs.append(_pair_softmax_values(scores[2 * pair], scores[2 * pair + 1], vm)[0])
    return jnp.concatenate(outs, axis=-1)


def _proj_a_kernel(x_ref, w_ref, ca_ref, cm_ref, cp_ref, o0_ref, o1_ref, o2_ref, qm_ref, stage_ref):
    xb = x_ref[...].astype(BF16)
    ca, cm, cp = ca_ref[...], cm_ref[...], cp_ref[...]
    blocks = GROUP_W // LANES
    for g, o_ref in enumerate((o0_ref, o1_ref, o2_ref)):
        dilation = DILATED_GROUPS[g][1]
        per = ROW_TILE // dilation
        for part in range(3):
            lo = part * MIX_W + g * GROUP_W
            h = _dot(xb, w_ref[:, lo:lo + GROUP_W].astype(BF16))
            for cb in range(blocks):
                t = h[:, cb * LANES:(cb + 1) * LANES]
                if part < 2:
                    t = t * ca + pltpu.roll(t, LANES - ROT_HALF, 1) * cm + pltpu.roll(t, ROT_HALF, 1) * cp
                if part == 0:
                    t = t * ATTN_SCALE
                if dilation == 1:
                    out_lo = part * GROUP_W + cb * LANES
                    o_ref[0, 0, :, out_lo:out_lo + LANES] = t.astype(BF16)
                else:
                    stage_ref[part * blocks + cb] = t
        if dilation > 1:
            for j in range(dilation):
                rows = [stage_ref[cb, pl.ds(j, per, stride=dilation), :] for cb in range(3 * blocks)]
                o_ref[0, j] = jnp.concatenate(rows, axis=1).astype(BF16)
    qm_ref[...] = (_dot(xb, w_ref[:, 3 * MIX_W:].astype(BF16)) * ATTN_SCALE).astype(BF16)


def _proj_a(x, w, ca, cm, cp, batch, seq):
    per_seq = seq // ROW_TILE
    tab = pl.BlockSpec((ROW_TILE, LANES), lambda b, i: (i, 0))
    grouped = lambda d: pl.BlockSpec((1, d, ROW_TILE // d, 3 * GROUP_W), lambda b, i: (b, 0, i, 0))
    return pl.pallas_call(
        _proj_a_kernel,
        grid=(batch, per_seq),
        in_specs=[pl.BlockSpec((ROW_TILE, D_MODEL), lambda b, i: (b * per_seq + i, 0)),
                  _resident((D_MODEL, 3 * MIX_W + MEM_W)), tab, tab, tab],
        out_specs=[grouped(d) for _, d in DILATED_GROUPS]
                  + [pl.BlockSpec((ROW_TILE, MEM_W), lambda b, i: (b * per_seq + i, 0))],
        out_shape=[jax.ShapeDtypeStruct((batch, d, seq // d, 3 * GROUP_W), BF16) for _, d in DILATED_GROUPS]
                  + [jax.ShapeDtypeStruct((batch * seq, MEM_W), BF16)],
        scratch_shapes=[pltpu.VMEM((3 * GROUP_W // LANES, ROW_TILE, LANES), F32)],
        compiler_params=_cparams("parallel", "parallel"),
        name="proj_a",
    )(x, w, ca, cm, cp)


def _dilated_kernel(cur_ref, prev_ref, o_ref, lse_ref, *, dilation, tile):
    first_key = jnp.where(pl.program_id(1) == 0, BAND, 0)
    qi = lax.broadcasted_iota(jnp.int32, (BAND, 2 * BAND), 0)
    ki = lax.broadcasted_iota(jnp.int32, (BAND, 2 * BAND), 1)
    band = (ki >= qi) & (ki <= qi + BAND)
    blocks = tile // BAND
    pairs = GROUP_W // LANES
    lane = lax.broadcasted_iota(jnp.int32, (BAND, LANES), 1)
    scores, values = [], []
    for sb in range(blocks):
        rows = slice(sb * BAND, (sb + 1) * BAND)
        before = prev_ref[0, 0] if sb == 0 else cur_ref[0, 0, (sb - 1) * BAND:sb * BAND, :]
        mask = band & (ki >= first_key) if sb == 0 else band
        kv = jnp.concatenate([before, cur_ref[0, 0, rows, :]], axis=0)
        for pair in range(pairs):
            lanes = slice(pair * LANES, (pair + 1) * LANES)
            q = cur_ref[0, 0, rows, lanes]
            k = kv[:, GROUP_W + pair * LANES:GROUP_W + (pair + 1) * LANES]
            scores.append([jnp.where(mask, _dot_nt(_head_of_pair(q, odd), k), NEG) for odd in (False, True)])
            values.append(kv[:, 2 * GROUP_W + pair * LANES:2 * GROUP_W + (pair + 1) * LANES])
    outs, lses = [], []
    for (s_even, s_odd), v in zip(scores, values):
        o, ((m0, l0), (m1, l1)) = _pair_softmax_values(s_even, s_odd, v)
        outs.append(o)
        lses.append(jnp.where(lane < HEAD_DIM, m0 + jnp.log(l0), m1 + jnp.log(l1)))
    j = pl.program_id(2)
    for c in range(pairs):
        o = jnp.concatenate([outs[sb * pairs + c] for sb in range(blocks)], axis=0)
        lse = jnp.concatenate([lses[sb * pairs + c] for sb in range(blocks)], axis=0)
        if dilation == 1:
            o_ref[c, 0] = o
            lse_ref[c, 0] = lse
        else:
            o_ref[c, 0, pl.ds(j, tile, stride=dilation), :] = o
            lse_ref[c, 0, pl.ds(j, tile, stride=dilation), :] = lse


def _dilated_group(qkv, batch, seq, dilation):
    length = seq // dilation
    tile = min(DIL_TILE, length)
    assert length % tile == 0 and tile % BAND == 0
    per = tile // BAND
    halves = GROUP_W // LANES
    out_block = pl.BlockSpec((halves, 1, dilation * tile, LANES), lambda b, t, j: (0, b, t, 0))
    return pl.pallas_call(
        functools.partial(_dilated_kernel, dilation=dilation, tile=tile),
        grid=(batch, length // tile, dilation),
        in_specs=[pl.BlockSpec((1, 1, tile, 3 * GROUP_W), lambda b, t, j: (b, j, t, 0)),
                  pl.BlockSpec((1, 1, BAND, 3 * GROUP_W), lambda b, t, j: (b, j, jnp.maximum(t * per - 1, 0), 0))],
        out_specs=[out_block, out_block],
        out_shape=[jax.ShapeDtypeStruct((halves, batch, seq, LANES), F32)] * 2,
        compiler_params=_cparams("parallel", "parallel", "arbitrary"),
        name=f"dilated_attn_d{dilation}",
    )(qkv, qkv)


def _out_a_kernel(x_ref, o0_ref, o1_ref, o2_ref, l0_ref, l1_ref, l2_ref, qm_ref, kv_ref, w_ref, g_ref, b_ref,
                  y_ref):
    scores = _memory_scores(qm_ref[...], kv_ref)
    halves = []
    for c in range(GROUP_W // LANES):
        l0, l1, l2 = l0_ref[c], l1_ref[c], l2_ref[c]
        m = jnp.maximum(jnp.maximum(l0, l1), l2)
        e0, e1, e2 = jnp.exp(l0 - m), jnp.exp(l1 - m), jnp.exp(l2 - m)
        halves.append((e0 * o0_ref[c] + e1 * o1_ref[c] + e2 * o2_ref[c]) / (e0 + e1 + e2))
    o_a = jnp.concatenate(halves, axis=1)
    mix = _dot(o_a.astype(BF16), w_ref[:GROUP_W, :].astype(BF16))
    o_m = _memory_values(scores, kv_ref)
    mix = mix + _dot(o_m.astype(BF16), w_ref[GROUP_W:, :].astype(BF16))
    y_ref[...] = _layer_norm(DEEPNORM_ALPHA * x_ref[...] + mix, g_ref[...], b_ref[...])


def _out_a(x, os, lses, qm, kv, w, g, b, seq):
    n = x.shape[0]
    per_seq = seq // ROW_TILE
    row = lambda w_: pl.BlockSpec((ROW_TILE, w_), lambda i: (i, 0))
    return pl.pallas_call(
        _out_a_kernel,
        grid=(n // ROW_TILE,),
        in_specs=[row(D_MODEL)] + [pl.BlockSpec((GROUP_W // LANES, ROW_TILE, LANES), lambda i: (0, i, 0))] * 6
                 + [row(MEM_W),
                  pl.BlockSpec((1, kv.shape[1], 2 * MEM_W), lambda i: (i // per_seq, 0, 0)),
                  _resident((GROUP_W + MEM_W, D_MODEL)), _resident((1, D_MODEL)), _resident((1, D_MODEL))],
        out_specs=row(D_MODEL),
        out_shape=jax.ShapeDtypeStruct((n, D_MODEL), F32),
        compiler_params=_cparams("parallel"),
        name="out_a_ln",
    )(x, *os, *lses, qm, kv, w, g, b)


def _decay_terms(xb, wf_ref, fb_ref, carry_ref):
    @pl.when(pl.program_id(1) == 0)
    def _():
        carry_ref[...] = jnp.zeros_like(carry_ref)

    f = _dot(xb, wf_ref[...])
    ft = f.T[:FOX_AUG] + fb_ref[...]
    logf = jnp.minimum(ft, 0.0) - jnp.log1p(jnp.exp(-jnp.abs(ft)))
    lane = lax.broadcasted_iota(jnp.int32, logf.shape, 1)
    c = logf
    shift = 1
    while shift < ROW_TILE:
        c = c + jnp.where(lane >= shift, pltpu.roll(c, shift, 1), 0.0)
        shift *= 2
    c = c + carry_ref[...]
    carry_ref[...] = c[:, ROW_TILE - 1:ROW_TILE]
    c = c * LOG2_E
    hi = c.astype(BF16)
    r1 = c - hi.astype(F32)
    mid = r1.astype(BF16)
    lo = (r1 - mid.astype(F32)).astype(BF16)
    ones = jnp.ones((FOX_AUG, ROW_TILE), BF16)
    zeros = jnp.zeros((LANES - 4 * FOX_AUG, ROW_TILE), BF16)
    return jnp.concatenate([hi, mid, lo, ones, zeros], axis=0)


def _proj_b_kernel(x_ref, wf_ref, fb_ref, wqt_ref, wk_ref, wvt_ref, wqm_ref, pq_ref, pk_ref,
                   qt_ref, k_ref, vt_ref, qm_ref, carry_ref):
    xb = x_ref[...].astype(BF16)
    stack = _decay_terms(xb, wf_ref, fb_ref, carry_ref)
    qt = _dot_nt(wqt_ref[...], xb)
    vt = _dot_nt(wvt_ref[...], xb)
    kk = _dot(xb, wk_ref[...])
    aug = _dot(pq_ref[...], stack)
    bias = _dot(stack.astype(F32).T.astype(BF16), pk_ref[...])
    low_lanes = lax.broadcasted_iota(jnp.int32, (ROW_TILE, LANES), 1) < HEAD_DIM
    for pair in range(MIX_W // LANES):
        lanes = slice(pair * LANES, (pair + 1) * LANES)
        k_ref[0, 2 * pair] = jnp.where(low_lanes, kk[:, lanes], bias[:, lanes]).astype(BF16)
        k_ref[0, 2 * pair + 1] = jnp.where(low_lanes, bias[:, lanes], kk[:, lanes]).astype(BF16)
    row = lax.broadcasted_iota(jnp.int32, (FOX_AUG, ROW_TILE), 0)
    ones_row = jnp.where(row == 0, 1.0, 0.0).astype(BF16)
    for h in range(N_MIX_HEADS):
        hs = slice(h * HEAD_DIM, (h + 1) * HEAD_DIM)
        q_rows, aug_rows = (slice(0, HEAD_DIM), slice(HEAD_DIM, LANES)) if h % 2 == 0 else \
                           (slice(HEAD_DIM, LANES), slice(0, HEAD_DIM))
        qt_ref[0, h, q_rows, :] = qt[hs].astype(BF16)
        qt_ref[0, h, aug_rows, :] = aug[hs].astype(BF16)
        vt_ref[0, h, :HEAD_DIM, :] = vt[hs].astype(BF16)
        vt_ref[0, h, HEAD_DIM:, :] = ones_row
    qm_ref[...] = _dot(xb, wqm_ref[...]).astype(BF16)


def _proj_b(x, wf, fb, wqt, wk, wvt, wqm, pq, pk, batch, seq):
    per_seq = seq // ROW_TILE
    v_rows = HEAD_DIM + FOX_AUG
    pairs = MIX_W // LANES
    rows = lambda w_: pl.BlockSpec((ROW_TILE, w_), lambda b, i: (b * per_seq + i, 0))
    return pl.pallas_call(
        _proj_b_kernel,
        grid=(batch, per_seq),
        in_specs=[rows(D_MODEL), _resident(wf.shape), _resident(fb.shape),
                  _resident(wqt.shape), _resident(wk.shape), _resident(wvt.shape), _resident(wqm.shape),
                  _resident(pq.shape), _resident(pk.shape)],
        out_specs=[pl.BlockSpec((1, N_MIX_HEADS, LANES, ROW_TILE), lambda b, i: (b, 0, 0, i)),
                   pl.BlockSpec((1, N_MIX_HEADS, ROW_TILE, LANES), lambda b, i: (b, 0, i, 0)),
                   pl.BlockSpec((1, N_MIX_HEADS, v_rows, ROW_TILE), lambda b, i: (b, 0, 0, i)),
                   rows(MEM_W)],
        out_shape=[jax.ShapeDtypeStruct((batch, N_MIX_HEADS, LANES, seq), BF16),
                   jax.ShapeDtypeStruct((batch, N_MIX_HEADS, seq, LANES), BF16),
                   jax.ShapeDtypeStruct((batch, N_MIX_HEADS, v_rows, seq), BF16),
                   jax.ShapeDtypeStruct((batch * seq, MEM_W), BF16)],
        scratch_shapes=[pltpu.VMEM((FOX_AUG, 1), F32)],
        compiler_params=_cparams("parallel", "arbitrary"),
        name="proj_b",
    )(x, wf, fb, wqt, wk, wvt, wqm, pq, pk)


def _fox_kernel(qt_ref, k_ref, vt_ref, o_ref, s_ref):
    seq = k_ref.shape[2]
    heads = k_ref.shape[1]
    key_pos = lax.broadcasted_iota(jnp.int32, (FOX_TILE, FOX_TILE), 0)
    qry_pos = lax.broadcasted_iota(jnp.int32, (FOX_TILE, FOX_TILE), 1)
    future = key_pos > qry_pos
    items = [(qi, h) for qi in range(seq // FOX_TILE) for h in range(heads)]

    def scores(i):
        qi, h = items[i]
        q = qt_ref[0, h, :, qi * FOX_TILE:(qi + 1) * FOX_TILE]
        m = None
        for kc in range(qi + 1):
            rows = slice(kc * FOX_TILE, (kc + 1) * FOX_TILE)
            s = _dot(k_ref[0, h, rows, :], q)
            if kc == qi:
                s = jnp.where(future, NEG, s)
            s_ref[i % FOX_SLOTS, rows, :] = s
            mc = jnp.max(s, axis=0, keepdims=True)
            m = mc if m is None else jnp.maximum(m, mc)
        return m

    ahead = FOX_SLOTS - 1
    maxima = [scores(i) for i in range(min(ahead, len(items)))]
    outs = []
    for i, (qi, h) in enumerate(items):
        if i + ahead < len(items):
            maxima.append(scores(i + ahead))
        m = maxima[i]
        acc = None
        for kc in range(qi + 1):
            rows = slice(kc * FOX_TILE, (kc + 1) * FOX_TILE)
            p = jnp.exp2(s_ref[i % FOX_SLOTS, rows, :] - m).astype(BF16)
            part = _dot(vt_ref[0, h, :, rows], p)
            acc = part if acc is None else acc + part
        outs.append(acc[:HEAD_DIM] / acc[HEAD_DIM:HEAD_DIM + 1])
        if h == heads - 1:
            o_ref[0, qi * FOX_TILE:(qi + 1) * FOX_TILE, :] = jnp.concatenate(outs, axis=0).T.astype(BF16)
            outs = []


def _fox(qt, kk, vt, batch, seq):
    pair = LANES // HEAD_DIM
    v_rows = HEAD_DIM + FOX_AUG
    return pl.pallas_call(
        _fox_kernel,
        grid=(batch, N_MIX_HEADS // pair),
        in_specs=[pl.BlockSpec((1, pair, LANES, seq), lambda b, h: (b, h, 0, 0)),
                  pl.BlockSpec((1, pair, seq, LANES), lambda b, h: (b, h, 0, 0)),
                  pl.BlockSpec((1, pair, v_rows, seq), lambda b, h: (b, h, 0, 0))],
        out_specs=pl.BlockSpec((1, seq, LANES), lambda b, h: (b, 0, h)),
        out_shape=jax.ShapeDtypeStruct((batch, seq, MIX_W), BF16),
        scratch_shapes=[pltpu.VMEM((FOX_SLOTS, seq, FOX_TILE), F32)],
        compiler_params=_cparams("parallel", "parallel"),
        name="fox_attn",
    )(qt, kk, vt)


def _out_b_kernel(x_ref, o_ref, qm_ref, kv_ref, w_ref, g_ref, b_ref, y_ref):
    scores = _memory_scores(qm_ref[...], kv_ref)
    mix = _dot(o_ref[...], w_ref[:MIX_W, :].astype(BF16))
    o_m = _memory_values(scores, kv_ref)
    mix = mix + _dot(o_m.astype(BF16), w_ref[MIX_W:, :].astype(BF16))
    y_ref[...] = _layer_norm(DEEPNORM_ALPHA * x_ref[...] + mix, g_ref[...], b_ref[...])


def _out_b(x, o, qm, kv, w, g, b, seq):
    n = x.shape[0]
    per_seq = seq // ROW_TILE
    row = lambda w_: pl.BlockSpec((ROW_TILE, w_), lambda i: (i, 0))
    return pl.pallas_call(
        _out_b_kernel,
        grid=(n // ROW_TILE,),
        in_specs=[row(D_MODEL), row(MIX_W), row(MEM_W),
                  pl.BlockSpec((1, kv.shape[1], 2 * MEM_W), lambda i: (i // per_seq, 0, 0)),
                  _resident((MIX_W + MEM_W, D_MODEL)), _resident((1, D_MODEL)), _resident((1, D_MODEL))],
        out_specs=row(D_MODEL),
        out_shape=jax.ShapeDtypeStruct((n, D_MODEL), F32),
        compiler_params=_cparams("parallel"),
        name="out_b_ln",
    )(x, o, qm, kv, w, g, b)


def _rope_tables(seq):
    pos = jnp.arange(seq, dtype=F32)
    inv_freq = 1.0 / (ROPE_THETA ** (jnp.arange(ROT_HALF, dtype=F32) / ROT_HALF))
    ang = pos[:, None] * inv_freq[None, :]
    cos, sin = jnp.cos(ang), jnp.sin(ang)
    ones = jnp.ones((seq, HEAD_DIM - ROT_DIM), F32)
    zeros = jnp.zeros((seq, HEAD_DIM - ROT_DIM), F32)
    zero_half = jnp.zeros((seq, ROT_HALF), F32)
    head = lambda parts: jnp.concatenate(parts * (LANES // HEAD_DIM), axis=1)
    return head([cos, cos, ones]), head([-sin, zero_half, zeros]), head([zero_half, sin, zeros])


def _placement_matrices():
    h = jnp.arange(N_MIX_HEADS)
    pq = jnp.zeros((MIX_W, LANES), F32)
    pk = jnp.zeros((LANES, MIX_W), F32)
    base = (h // 2) * LANES + jnp.where(h % 2 == 0, HEAD_DIM, 0)
    for part in range(3):
        pq = pq.at[h * HEAD_DIM + part, part * FOX_AUG + h].set(1.0)
        pq = pq.at[h * HEAD_DIM + 3 + part, 3 * FOX_AUG].set(1.0)
        pk = pk.at[3 * FOX_AUG, base + part].set(1.0)
        pk = pk.at[part * FOX_AUG + h, base + 3 + part].set(-1.0)
    return pq.astype(BF16), pk.astype(BF16)


def _layer_a_mixer(xf, mem, w_in, w_mem_kv, w_out, g, b, batch, seq):
    n = batch * seq
    ca, cm, cp = _rope_tables(seq)
    qkv0, qkv1, qkv2, qm = _proj_a(xf, w_in, ca, cm, cp, batch, seq)
    os, lses = [], []
    for qkv, (_, dilation) in zip((qkv0, qkv1, qkv2), DILATED_GROUPS):
        o, lse = _dilated_group(qkv, batch, seq, dilation)
        os.append(o.reshape(GROUP_W // LANES, n, LANES))
        lses.append(lse.reshape(GROUP_W // LANES, n, LANES))
    kv = _memkv(mem.reshape(batch * mem.shape[1], D_MODEL), w_mem_kv)
    kv = kv.reshape(batch, mem.shape[1], 2 * MEM_W)
    return _out_a(xf, os, lses, qm, kv, w_out, g, b, seq)


def _layer_b_mixer(xf, mem, w_in, forget_bias, w_mem_kv, w_out, g, b, batch, seq):
    n = batch * seq
    wf = jnp.pad(w_in[:, 3 * MIX_W:3 * MIX_W + N_MIX_HEADS], ((0, 0), (0, LANES - N_MIX_HEADS))).astype(BF16)
    fb = jnp.pad(forget_bias.astype(F32), (0, FOX_AUG - N_MIX_HEADS)).reshape(FOX_AUG, 1)
    pq, pk = _placement_matrices()
    wqt = (w_in[:, :MIX_W] * (ATTN_SCALE * LOG2_E)).T.astype(BF16)
    wk = w_in[:, MIX_W:2 * MIX_W].astype(BF16)
    wvt = w_in[:, 2 * MIX_W:3 * MIX_W].T.astype(BF16)
    wqm = (w_in[:, 3 * MIX_W + N_MIX_HEADS:] * ATTN_SCALE).astype(BF16)
    qt, kk, vt, qm = _proj_b(xf, wf, fb, wqt, wk, wvt, wqm, pq, pk, batch, seq)
    o = _fox(qt, kk, vt, batch, seq).reshape(n, MIX_W)
    kv = _memkv(mem.reshape(batch * mem.shape[1], D_MODEL), w_mem_kv)
    kv = kv.reshape(batch, mem.shape[1], 2 * MEM_W)
    return _out_b(xf, o, qm, kv, w_out, g, b, seq)


def kernel(x, mem, ffn1_w_gate_up, ffn1_w_down, ffn2_w_gate_up, ffn2_w_down, ln_gain, ln_bias, mem_w_kv,
           a_w_in, a_w_out, b_w_in, b_forget_bias, b_w_out):
    batch, seq, _ = x.shape
    n = batch * seq
    assert seq % ROW_TILE == 0 and all(seq % (d * BAND) == 0 and ROW_TILE % d == 0 for _, d in DILATED_GROUPS)
    xf = x.reshape(n, D_MODEL)
    gain = lambda i, j: ln_gain[i, j].reshape(1, D_MODEL)
    bias = lambda i, j: ln_bias[i, j].reshape(1, D_MODEL)
    ffn = lambda v, wgu, wd, i, j: _ffn(v, wgu, wd, i, gain(i, j), bias(i, j))

    xf = ffn(xf, ffn1_w_gate_up, ffn1_w_down, 0, 0)
    xf = _layer_a_mixer(xf, mem, a_w_in[0], mem_w_kv[0], a_w_out[0], gain(0, 1), bias(0, 1), batch, seq)
    xf = ffn(xf, ffn2_w_gate_up, ffn2_w_down, 0, 2)

    xf = ffn(xf, ffn1_w_gate_up, ffn1_w_down, 1, 0)
    xf = _layer_b_mixer(xf, mem, b_w_in[0], b_forget_bias[0], mem_w_kv[1], b_w_out[0], gain(1, 1), bias(1, 1),
                        batch, seq)
    xf = ffn(xf, ffn2_w_gate_up, ffn2_w_down, 1, 2)
    return xf.reshape(batch, seq, D_MODEL)
```

```python
import functools

import jax
import jax.numpy as jnp
from jax import lax
from jax.experimental import pallas as pl
from jax.experimental.pallas import tpu as pltpu

D_MODEL = 1024
DEPTH = 2
HEAD_DIM = 64
N_MIX_HEADS = 12
N_MEM_HEADS = 4
DILATED_GROUPS = ((128, 1), (512, 4), (2048, 16))
HEADS_PER_GROUP = N_MIX_HEADS // len(DILATED_GROUPS)
ROT_DIM = HEAD_DIM // 4
ROT_HALF = ROT_DIM // 2
ROPE_THETA = 500000.0
D_FF = 2816
BAND = 128
DEEPNORM_ALPHA = (2 * DEPTH) ** 0.25
LN_EPS = 1e-5
MIX_W = N_MIX_HEADS * HEAD_DIM
MEM_W = N_MEM_HEADS * HEAD_DIM
GROUP_W = HEADS_PER_GROUP * HEAD_DIM
ATTN_SCALE = HEAD_DIM ** -0.5

LANES = 128
VMEM_LIMIT = 56 * 1024 * 1024

ROW_TILE = 1024
FFN_TILE = 512
FF_CHUNK = 256
FOX_TILE = 512
FOX_AUG = 16
FOX_SLOTS = 2
DIL_TILE = 1024

F32 = jnp.float32
BF16 = jnp.bfloat16
NEG = -1e30
LOG2_E = 1.4426950408889634


def _cparams(*sem):
    return pltpu.CompilerParams(dimension_semantics=sem, vmem_limit_bytes=VMEM_LIMIT)


def _resident(shape):
    return pl.BlockSpec(shape, lambda *_: (0,) * len(shape), pipeline_mode=pl.Buffered(1))


def _dot(a, b):
    return jnp.dot(a, b, preferred_element_type=F32)


def _dot_nt(a, b):
    return lax.dot_general(a, b, (((1,), (1,)), ((), ())), preferred_element_type=F32)


def _layer_norm(z, g, b):
    mu = jnp.mean(z, axis=-1, keepdims=True)
    zc = z - mu
    var = jnp.mean(zc * zc, axis=-1, keepdims=True)
    return zc * lax.rsqrt(var + LN_EPS) * g + b


def _ffn_kernel(x_ref, wgu_ref, wd_ref, g_ref, b_ref, o_ref, h_ref):
    x = x_ref[...]
    xb = x.astype(BF16)
    for c in range(D_FF // FF_CHUNK):
        lo = c * FF_CHUNK
        gate = _dot(xb, wgu_ref[0, :, lo:lo + FF_CHUNK].astype(BF16))
        up = _dot(xb, wgu_ref[0, :, D_FF + lo:D_FF + lo + FF_CHUNK].astype(BF16))
        h_ref[:, lo:lo + FF_CHUNK] = (gate * jax.nn.sigmoid(gate) * up).astype(BF16)
    y = _dot(h_ref[...], wd_ref[0].astype(BF16))
    o_ref[...] = _layer_norm(DEEPNORM_ALPHA * x + 0.5 * y, g_ref[...], b_ref[...])


def _ffn(x, wgu_all, wd_all, layer, g, b):
    n = x.shape[0]
    row = pl.BlockSpec((FFN_TILE, D_MODEL), lambda i: (i, 0))
    one_layer = lambda shape: pl.BlockSpec((1,) + shape, lambda i: (layer, 0, 0), pipeline_mode=pl.Buffered(1))
    return pl.pallas_call(
        _ffn_kernel,
        grid=(n // FFN_TILE,),
        in_specs=[row, one_layer((D_MODEL, 2 * D_FF)), one_layer((D_FF, D_MODEL)),
                  _resident((1, D_MODEL)), _resident((1, D_MODEL))],
        out_specs=row,
        out_shape=jax.ShapeDtypeStruct((n, D_MODEL), F32),
        scratch_shapes=[pltpu.VMEM((FFN_TILE, D_FF), BF16)],
        compiler_params=_cparams("parallel"),
        name="ffn_ln",
    )(x, wgu_all, wd_all, g, b)


def _memkv_kernel(m_ref, w_ref, o_ref):
    o_ref[...] = _dot(m_ref[...].astype(BF16), w_ref[...].astype(BF16)).astype(BF16)


def _memkv(mem2d, w):
    n = mem2d.shape[0]
    tile = min(ROW_TILE, n)
    return pl.pallas_call(
        _memkv_kernel,
        grid=(n // tile,),
        in_specs=[pl.BlockSpec((tile, D_MODEL), lambda i: (i, 0)), _resident((D_MODEL, 2 * MEM_W))],
        out_specs=pl.BlockSpec((tile, 2 * MEM_W), lambda i: (i, 0)),
        out_shape=jax.ShapeDtypeStruct((n, 2 * MEM_W), BF16),
        compiler_params=_cparams("parallel"),
        name="mem_kv",
    )(mem2d, w)


def _head_of_pair(block, odd):
    lane = lax.broadcasted_iota(jnp.int32, block.shape, 1)
    keep = lane >= HEAD_DIM if odd else lane < HEAD_DIM
    return jnp.where(keep, block, jnp.zeros_like(block))


def _pair_softmax_values(s_even, s_odd, v_pair):
    ones = jnp.ones((v_pair.shape[0], LANES), BF16)
    res, stats = [], []
    for s in (s_even, s_odd):
        m = jnp.max(s, axis=-1, keepdims=True)
        p = jnp.exp(s - m).astype(BF16)
        l = _dot(p, ones)
        res.append(_dot(p, v_pair) / l)
        stats.append((m, l))
    lane = lax.broadcasted_iota(jnp.int32, res[0].shape, 1)
    return jnp.where(lane < HEAD_DIM, res[0], res[1]), stats


def _memory_scores(qm, kv_ref):
    scores = []
    for pair in range(MEM_W // LANES):
        lanes = slice(pair * LANES, (pair + 1) * LANES)
        km = kv_ref[0, :, lanes]
        scores += [_dot_nt(_head_of_pair(qm[:, lanes], odd), km) for odd in (False, True)]
    return scores


def _memory_values(scores, kv_ref):
    outs = []
    for pair in range(MEM_W // LANES):
        vm = kv_ref[0, :, MEM_W + pair * LANES:MEM_W + (pair + 1) * LANES]
        outs.append(_pair_softmax_values(scores[2 * pair], scores[2 * pair + 1], vm)[0])
    return jnp.concatenate(outs, axis=-1)


def _proj_a_kernel(x_ref, w_ref, ca_ref, cm_ref, cp_ref, o0_ref, o1_ref, o2_ref, qm_ref, stage_ref):
    xb = x_ref[...].astype(BF16)
    ca, cm, cp = ca_ref[...], cm_ref[...], cp_ref[...]
    blocks = GROUP_W // LANES
    for g, o_ref in enumerate((o0_ref, o1_ref, o2_ref)):
        dilation = DILATED_GROUPS[g][1]
        per = ROW_TILE // dilation
        for part in range(3):
            lo = part * MIX_W + g * GROUP_W
            h = _dot(xb, w_ref[:, lo:lo + GROUP_W].astype(BF16))
            for cb in range(blocks):
                t = h[:, cb * LANES:(cb + 1) * LANES]
                if part < 2:
                    t = t * ca + pltpu.roll(t, LANES - ROT_HALF, 1) * cm + pltpu.roll(t, ROT_HALF, 1) * cp
                if part == 0:
                    t = t * ATTN_SCALE
                if dilation == 1:
                    out_lo = part * GROUP_W + cb * LANES
                    o_ref[0, 0, :, out_lo:out_lo + LANES] = t.astype(BF16)
                else:
                    stage_ref[part * blocks + cb] = t
        if dilation > 1:
            for j in range(dilation):
                rows = [stage_ref[cb, pl.ds(j, per, stride=dilation), :] for cb in range(3 * blocks)]
                o_ref[0, j] = jnp.concatenate(rows, axis=1).astype(BF16)
    qm_ref[...] = (_dot(xb, w_ref[:, 3 * MIX_W:].astype(BF16)) * ATTN_SCALE).astype(BF16)


def _proj_a(x, w, ca, cm, cp, batch, seq):
    per_seq = seq // ROW_TILE
    tab = pl.BlockSpec((ROW_TILE, LANES), lambda b, i: (i, 0))
    grouped = lambda d: pl.BlockSpec((1, d, ROW_TILE // d, 3 * GROUP_W), lambda b, i: (b, 0, i, 0))
    return pl.pallas_call(
        _proj_a_kernel,
        grid=(batch, per_seq),
        in_specs=[pl.BlockSpec((ROW_TILE, D_MODEL), lambda b, i: (b * per_seq + i, 0)),
                  _resident((D_MODEL, 3 * MIX_W + MEM_W)), tab, tab, tab],
        out_specs=[grouped(d) for _, d in DILATED_GROUPS]
                  + [pl.BlockSpec((ROW_TILE, MEM_W), lambda b, i: (b * per_seq + i, 0))],
        out_shape=[jax.ShapeDtypeStruct((batch, d, seq // d, 3 * GROUP_W), BF16) for _, d in DILATED_GROUPS]
                  + [jax.ShapeDtypeStruct((batch * seq, MEM_W), BF16)],
        scratch_shapes=[pltpu.VMEM((3 * GROUP_W // LANES, ROW_TILE, LANES), F32)],
        compiler_params=_cparams("parallel", "parallel"),
        name="proj_a",
    )(x, w, ca, cm, cp)


def _dilated_kernel(cur_ref, prev_ref, o_ref, lse_ref, *, dilation, tile):
    first_key = jnp.where(pl.program_id(1) == 0, BAND, 0)
    qi = lax.broadcasted_iota(jnp.int32, (BAND, 2 * BAND), 0)
    ki = lax.broadcasted_iota(jnp.int32, (BAND, 2 * BAND), 1)
    band = (ki >= qi) & (ki <= qi + BAND)
    blocks = tile // BAND
    pairs = GROUP_W // LANES
    lane = lax.broadcasted_iota(jnp.int32, (BAND, LANES), 1)
    scores, values = [], []
    for sb in range(blocks):
        rows = slice(sb * BAND, (sb + 1) * BAND)
        before = prev_ref[0, 0] if sb == 0 else cur_ref[0, 0, (sb - 1) * BAND:sb * BAND, :]
        mask = band & (ki >= first_key) if sb == 0 else band
        kv = jnp.concatenate([before, cur_ref[0, 0, rows, :]], axis=0)
        for pair in range(pairs):
            lanes = slice(pair * LANES, (pair + 1) * LANES)
            q = cur_ref[0, 0, rows, lanes]
            k = kv[:, GROUP_W + pair * LANES:GROUP_W + (pair + 1) * LANES]
            scores.append([jnp.where(mask, _dot_nt(_head_of_pair(q, odd), k), NEG) for odd in (False, True)])
            values.append(kv[:, 2 * GROUP_W + pair * LANES:2 * GROUP_W + (pair + 1) * LANES])
    outs, lses = [], []
    for (s_even, s_odd), v in zip(scores, values):
        o, ((m0, l0), (m1, l1)) = _pair_softmax_values(s_even, s_odd, v)
        outs.append(o)
        lses.append(jnp.where(lane < HEAD_DIM, m0 + jnp.log(l0), m1 + jnp.log(l1)))
    j = pl.program_id(2)
    for c in range(pairs):
        o = jnp.concatenate([outs[sb * pairs + c] for sb in range(blocks)], axis=0)
        lse = jnp.concatenate([lses[sb * pairs + c] for sb in range(blocks)], axis=0)
        if dilation == 1:
            o_ref[c, 0] = o
            lse_ref[c, 0] = lse
        else:
            o_ref[c, 0, pl.ds(j, tile, stride=dilation), :] = o
            lse_ref[c, 0, pl.ds(j, tile, stride=dilation), :] = lse


def _dilated_group(qkv, batch, seq, dilation):
    length = seq // dilation
    tile = min(DIL_TILE, length)
    assert length % tile == 0 and tile % BAND == 0
    per = tile // BAND
    halves = GROUP_W // LANES
    out_block = pl.BlockSpec((halves, 1, dilation * tile, LANES), lambda b, t, j: (0, b, t, 0))
    return pl.pallas_call(
        functools.partial(_dilated_kernel, dilation=dilation, tile=tile),
        grid=(batch, length // tile, dilation),
        in_specs=[pl.BlockSpec((1, 1, tile, 3 * GROUP_W), lambda b, t, j: (b, j, t, 0)),
                  pl.BlockSpec((1, 1, BAND, 3 * GROUP_W), lambda b, t, j: (b, j, jnp.maximum(t * per - 1, 0), 0))],
        out_specs=[out_block, out_block],
        out_shape=[jax.ShapeDtypeStruct((halves, batch, seq, LANES), F32)] * 2,
        compiler_params=_cparams("parallel", "parallel", "arbitrary"),
        name=f"dilated_attn_d{dilation}",
    )(qkv, qkv)


def _out_a_kernel(x_ref, o0_ref, o1_ref, o2_ref, l0_ref, l1_ref, l2_ref, qm_ref, kv_ref, w_ref, g_ref, b_ref,
                  y_ref):
    scores = _memory_scores(qm_ref[...], kv_ref)
    halves = []
    for c in range(GROUP_W // LANES):
        l0, l1, l2 = l0_ref[c], l1_ref[c], l2_ref[c]
        m = jnp.maximum(jnp.maximum(l0, l1), l2)
        e0, e1, e2 = jnp.exp(l0 - m), jnp.exp(l1 - m), jnp.exp(l2 - m)
        halves.append((e0 * o0_ref[c] + e1 * o1_ref[c] + e2 * o2_ref[c]) / (e0 + e1 + e2))
    o_a = jnp.concatenate(halves, axis=1)
    mix = _dot(o_a.astype(BF16), w_ref[:GROUP_W, :].astype(BF16))
    o_m = _memory_values(scores, kv_ref)
    mix = mix + _dot(o_m.astype(BF16), w_ref[GROUP_W:, :].astype(BF16))
    y_ref[...] = _layer_norm(DEEPNORM_ALPHA * x_ref[...] + mix, g_ref[...], b_ref[...])


def _out_a(x, os, lses, qm, kv, w, g, b, seq):
    n = x.shape[0]
    per_seq = seq // ROW_TILE
    row = lambda w_: pl.BlockSpec((ROW_TILE, w_), lambda i: (i, 0))
    return pl.pallas_call(
        _out_a_kernel,
        grid=(n // ROW_TILE,),
        in_specs=[row(D_MODEL)] + [pl.BlockSpec((GROUP_W // LANES, ROW_TILE, LANES), lambda i: (0, i, 0))] * 6
                 + [row(MEM_W),
                  pl.BlockSpec((1, kv.shape[1], 2 * MEM_W), lambda i: (i // per_seq, 0, 0)),
                  _resident((GROUP_W + MEM_W, D_MODEL)), _resident((1, D_MODEL)), _resident((1, D_MODEL))],
        out_specs=row(D_MODEL),
        out_shape=jax.ShapeDtypeStruct((n, D_MODEL), F32),
        compiler_params=_cparams("parallel"),
        name="out_a_ln",
    )(x, *os, *lses, qm, kv, w, g, b)


def _decay_terms(xb, wf_ref, fb_ref, carry_ref):
    @pl.when(pl.program_id(1) == 0)
    def _():
        carry_ref[...] = jnp.zeros_like(carry_ref)

    f = _dot(xb, wf_ref[...])
    ft = f.T[:FOX_AUG] + fb_ref[...]
    logf = jnp.minimum(ft, 0.0) - jnp.log1p(jnp.exp(-jnp.abs(ft)))
    lane = lax.broadcasted_iota(jnp.int32, logf.shape, 1)
    c = logf
    shift = 1
    while shift < ROW_TILE:
        c = c + jnp.where(lane >= shift, pltpu.roll(c, shift, 1), 0.0)
        shift *= 2
    c = c + carry_ref[...]
    carry_ref[...] = c[:, ROW_TILE - 1:ROW_TILE]
    c = c * LOG2_E
    hi = c.astype(BF16)
    r1 = c - hi.astype(F32)
    mid = r1.astype(BF16)
    lo = (r1 - mid.astype(F32)).astype(BF16)
    ones = jnp.ones((FOX_AUG, ROW_TILE), BF16)
    zeros = jnp.zeros((LANES - 4 * FOX_AUG, ROW_TILE), BF16)
    return jnp.concatenate([hi, mid, lo, ones, zeros], axis=0)


def _proj_b_kernel(x_ref, wf_ref, fb_ref, wqt_ref, wk_ref, wvt_ref, wqm_ref, pq_ref, pk_ref,
                   qt_ref, k_ref, vt_ref, qm_ref, carry_ref):
    xb = x_ref[...].astype(BF16)
    stack = _decay_terms(xb, wf_ref, fb_ref, carry_ref)
    qt = _dot_nt(wqt_ref[...], xb)
    vt = _dot_nt(wvt_ref[...], xb)
    kk = _dot(xb, wk_ref[...])
    aug = _dot(pq_ref[...], stack)
    bias = _dot(stack.astype(F32).T.astype(BF16), pk_ref[...])
    low_lanes = lax.broadcasted_iota(jnp.int32, (ROW_TILE, LANES), 1) < HEAD_DIM
    for pair in range(MIX_W // LANES):
        lanes = slice(pair * LANES, (pair + 1) * LANES)
        k_ref[0, 2 * pair] = jnp.where(low_lanes, kk[:, lanes], bias[:, lanes]).astype(BF16)
        k_ref[0, 2 * pair + 1] = jnp.where(low_lanes, bias[:, lanes], kk[:, lanes]).astype(BF16)
    row = lax.broadcasted_iota(jnp.int32, (FOX_AUG, ROW_TILE), 0)
    ones_row = jnp.where(row == 0, 1.0, 0.0).astype(BF16)
    for h in range(N_MIX_HEADS):
        hs = slice(h * HEAD_DIM, (h + 1) * HEAD_DIM)
        q_rows, aug_rows = (slice(0, HEAD_DIM), slice(HEAD_DIM, LANES)) if h % 2 == 0 else \
                           (slice(HEAD_DIM, LANES), slice(0, HEAD_DIM))
        qt_ref[0, h, q_rows, :] = qt[hs].astype(BF16)
        qt_ref[0, h, aug_rows, :] = aug[hs].astype(BF16)
        vt_ref[0, h, :HEAD_DIM, :] = vt[hs].astype(BF16)
        vt_ref[0, h, HEAD_DIM:, :] = ones_row
    qm_ref[...] = _dot(xb, wqm_ref[...]).astype(BF16)


def _proj_b(x, wf, fb, wqt, wk, wvt, wqm, pq, pk, batch, seq):
    per_seq = seq // ROW_TILE
    v_rows = HEAD_DIM + FOX_AUG
    pairs = MIX_W // LANES
    rows = lambda w_: pl.BlockSpec((ROW_TILE, w_), lambda b, i: (b * per_seq + i, 0))
    return pl.pallas_call(
        _proj_b_kernel,
        grid=(batch, per_seq),
        in_specs=[rows(D_MODEL), _resident(wf.shape), _resident(fb.shape),
                  _resident(wqt.shape), _resident(wk.shape), _resident(wvt.shape), _resident(wqm.shape),
                  _resident(pq.shape), _resident(pk.shape)],
        out_specs=[pl.BlockSpec((1, N_MIX_HEADS, LANES, ROW_TILE), lambda b, i: (b, 0, 0, i)),
                   pl.BlockSpec((1, N_MIX_HEADS, ROW_TILE, LANES), lambda b, i: (b, 0, i, 0)),
                   pl.BlockSpec((1, N_MIX_HEADS, v_rows, ROW_TILE), lambda b, i: (b, 0, 0, i)),
                   rows(MEM_W)],
        out_shape=[jax.ShapeDtypeStruct((batch, N_MIX_HEADS, LANES, seq), BF16),
                   jax.ShapeDtypeStruct((batch, N_MIX_HEADS, seq, LANES), BF16),
                   jax.ShapeDtypeStruct((batch, N_MIX_HEADS, v_rows, seq), BF16),
                   jax.ShapeDtypeStruct((batch * seq, MEM_W), BF16)],
        scratch_shapes=[pltpu.VMEM((FOX_AUG, 1), F32)],
        compiler_params=_cparams("parallel", "arbitrary"),
        name="proj_b",
    )(x, wf, fb, wqt, wk, wvt, wqm, pq, pk)


def _fox_kernel(qt_ref, k_ref, vt_ref, o_ref, s_ref, p_ref):
    seq = k_ref.shape[2]
    heads = k_ref.shape[1]
    key_pos = lax.broadcasted_iota(jnp.int32, (FOX_TILE, FOX_TILE), 0)
    qry_pos = lax.broadcasted_iota(jnp.int32, (FOX_TILE, FOX_TILE), 1)
    future = key_pos > qry_pos
    items = [(qi, h) for qi in range(seq // FOX_TILE) for h in range(heads)]

    def scores(i):
        qi, h = items[i]
        q = qt_ref[0, h, :, qi * FOX_TILE:(qi + 1) * FOX_TILE]
        m = None
        for kc in range(qi + 1):
            rows = slice(kc * FOX_TILE, (kc + 1) * FOX_TILE)
            s = _dot(k_ref[0, h, rows, :], q)
            if kc == qi:
                s = jnp.where(future, NEG, s)
            s_ref[i % FOX_SLOTS, rows, :] = s
            mc = jnp.max(s, axis=0, keepdims=True)
            m = mc if m is None else jnp.maximum(m, mc)
        return m

    ahead = FOX_SLOTS - 1
    maxima = [scores(i) for i in range(min(ahead, len(items)))]
    outs = []
    for i, (qi, h) in enumerate(items):
        if i + ahead < len(items):
            maxima.append(scores(i + ahead))
        m = maxima[i]
        n_keys = (qi + 1) * FOX_TILE
        for kc in range(qi + 1):
            rows = slice(kc * FOX_TILE, (kc + 1) * FOX_TILE)
            p_ref[i % 2, rows, :] = jnp.exp2(s_ref[i % FOX_SLOTS, rows, :] - m).astype(BF16)
        acc = _dot(vt_ref[0, h, :, :n_keys], p_ref[i % 2, :n_keys, :])
        outs.append(acc[:HEAD_DIM] / acc[HEAD_DIM:HEAD_DIM + 1])
        if h == heads - 1:
            o_ref[0, qi * FOX_TILE:(qi + 1) * FOX_TILE, :] = jnp.concatenate(outs, axis=0).T.astype(BF16)
            outs = []


def _fox(qt, kk, vt, batch, seq):
    pair = LANES // HEAD_DIM
    v_rows = HEAD_DIM + FOX_AUG
    return pl.pallas_call(
        _fox_kernel,
        grid=(batch, N_MIX_HEADS // pair),
        in_specs=[pl.BlockSpec((1, pair, LANES, seq), lambda b, h: (b, h, 0, 0)),
                  pl.BlockSpec((1, pair, seq, LANES), lambda b, h: (b, h, 0, 0)),
                  pl.BlockSpec((1, pair, v_rows, seq), lambda b, h: (b, h, 0, 0))],
        out_specs=pl.BlockSpec((1, seq, LANES), lambda b, h: (b, 0, h)),
        out_shape=jax.ShapeDtypeStruct((batch, seq, MIX_W), BF16),
        scratch_shapes=[pltpu.VMEM((FOX_SLOTS, seq, FOX_TILE), F32), pltpu.VMEM((2, seq, FOX_TILE), BF16)],
        compiler_params=_cparams("parallel", "parallel"),
        name="fox_attn",
    )(qt, kk, vt)


def _out_b_kernel(x_ref, o_ref, qm_ref, kv_ref, w_ref, g_ref, b_ref, y_ref):
    scores = _memory_scores(qm_ref[...], kv_ref)
    mix = _dot(o_ref[...], w_ref[:MIX_W, :].astype(BF16))
    o_m = _memory_values(scores, kv_ref)
    mix = mix + _dot(o_m.astype(BF16), w_ref[MIX_W:, :].astype(BF16))
    y_ref[...] = _layer_norm(DEEPNORM_ALPHA * x_ref[...] + mix, g_ref[...], b_ref[...])


def _out_b(x, o, qm, kv, w, g, b, seq):
    n = x.shape[0]
    per_seq = seq // ROW_TILE
    row = lambda w_: pl.BlockSpec((ROW_TILE, w_), lambda i: (i, 0))
    return pl.pallas_call(
        _out_b_kernel,
        grid=(n // ROW_TILE,),
        in_specs=[row(D_MODEL), row(MIX_W), row(MEM_W),
                  pl.BlockSpec((1, kv.shape[1], 2 * MEM_W), lambda i: (i // per_seq, 0, 0)),
                  _resident((MIX_W + MEM_W, D_MODEL)), _resident((1, D_MODEL)), _resident((1, D_MODEL))],
        out_specs=row(D_MODEL),
        out_shape=jax.ShapeDtypeStruct((n, D_MODEL), F32),
        compiler_params=_cparams("parallel"),
        name="out_b_ln",
    )(x, o, qm, kv, w, g, b)


def _rope_tables(seq):
    pos = jnp.arange(seq, dtype=F32)
    inv_freq = 1.0 / (ROPE_THETA ** (jnp.arange(ROT_HALF, dtype=F32) / ROT_HALF))
    ang = pos[:, None] * inv_freq[None, :]
    cos, sin = jnp.cos(ang), jnp.sin(ang)
    ones = jnp.ones((seq, HEAD_DIM - ROT_DIM), F32)
    zeros = jnp.zeros((seq, HEAD_DIM - ROT_DIM), F32)
    zero_half = jnp.zeros((seq, ROT_HALF), F32)
    head = lambda parts: jnp.concatenate(parts * (LANES // HEAD_DIM), axis=1)
    return head([cos, cos, ones]), head([-sin, zero_half, zeros]), head([zero_half, sin, zeros])


def _placement_matrices():
    h = jnp.arange(N_MIX_HEADS)
    pq = jnp.zeros((MIX_W, LANES), F32)
    pk = jnp.zeros((LANES, MIX_W), F32)
    base = (h // 2) * LANES + jnp.where(h % 2 == 0, HEAD_DIM, 0)
    for part in range(3):
        pq = pq.at[h * HEAD_DIM + part, part * FOX_AUG + h].set(1.0)
        pq = pq.at[h * HEAD_DIM + 3 + part, 3 * FOX_AUG].set(1.0)
        pk = pk.at[3 * FOX_AUG, base + part].set(1.0)
        pk = pk.at[part * FOX_AUG + h, base + 3 + part].set(-1.0)
    return pq.astype(BF16), pk.astype(BF16)


def _layer_a_mixer(xf, mem, w_in, w_mem_kv, w_out, g, b, batch, seq):
    n = batch * seq
    ca, cm, cp = _rope_tables(seq)
    qkv0, qkv1, qkv2, qm = _proj_a(xf, w_in, ca, cm, cp, batch, seq)
    os, lses = [], []
    for qkv, (_, dilation) in zip((qkv0, qkv1, qkv2), DILATED_GROUPS):
        o, lse = _dilated_group(qkv, batch, seq, dilation)
        os.append(o.reshape(GROUP_W // LANES, n, LANES))
        lses.append(lse.reshape(GROUP_W // LANES, n, LANES))
    kv = _memkv(mem.reshape(batch * mem.shape[1], D_MODEL), w_mem_kv)
    kv = kv.reshape(batch, mem.shape[1], 2 * MEM_W)
    return _out_a(xf, os, lses, qm, kv, w_out, g, b, seq)


def _layer_b_mixer(xf, mem, w_in, forget_bias, w_mem_kv, w_out, g, b, batch, seq):
    n = batch * seq
    wf = jnp.pad(w_in[:, 3 * MIX_W:3 * MIX_W + N_MIX_HEADS], ((0, 0), (0, LANES - N_MIX_HEADS))).astype(BF16)
    fb = jnp.pad(forget_bias.astype(F32), (0, FOX_AUG - N_MIX_HEADS)).reshape(FOX_AUG, 1)
    pq, pk = _placement_matrices()
    wqt = (w_in[:, :MIX_W] * (ATTN_SCALE * LOG2_E)).T.astype(BF16)
    wk = w_in[:, MIX_W:2 * MIX_W].astype(BF16)
    wvt = w_in[:, 2 * MIX_W:3 * MIX_W].T.astype(BF16)
    wqm = (w_in[:, 3 * MIX_W + N_MIX_HEADS:] * ATTN_SCALE).astype(BF16)
    qt, kk, vt, qm = _proj_b(xf, wf, fb, wqt, wk, wvt, wqm, pq, pk, batch, seq)
    o = _fox(qt, kk, vt, batch, seq).reshape(n, MIX_W)
    kv = _memkv(mem.reshape(batch * mem.shape[1], D_MODEL), w_mem_kv)
    kv = kv.reshape(batch, mem.shape[1], 2 * MEM_W)
    return _out_b(xf, o, qm, kv, w_out, g, b, seq)


def kernel(x, mem, ffn1_w_gate_up, ffn1_w_down, ffn2_w_gate_up, ffn2_w_down, ln_gain, ln_bias, mem_w_kv,
           a_w_in, a_w_out, b_w_in, b_forget_bias, b_w_out):
    batch, seq, _ = x.shape
    n = batch * seq
    assert seq % ROW_TILE == 0 and all(seq % (d * BAND) == 0 and ROW_TILE % d == 0 for _, d in DILATED_GROUPS)
    xf = x.reshape(n, D_MODEL)
    gain = lambda i, j: ln_gain[i, j].reshape(1, D_MODEL)
    bias = lambda i, j: ln_bias[i, j].reshape(1, D_MODEL)
    ffn = lambda v, wgu, wd, i, j: _ffn(v, wgu, wd, i, gain(i, j), bias(i, j))

    xf = ffn(xf, ffn1_w_gate_up, ffn1_w_down, 0, 0)
    xf = _layer_a_mixer(xf, mem, a_w_in[0], mem_w_kv[0], a_w_out[0], gain(0, 1), bias(0, 1), batch, seq)
    xf = ffn(xf, ffn2_w_gate_up, ffn2_w_down, 0, 2)

    xf = ffn(xf, ffn1_w_gate_up, ffn1_w_down, 1, 0)
    xf = _layer_b_mixer(xf, mem, b_w_in[0], b_forget_bias[0], mem_w_kv[1], b_w_out[0], gain(1, 1), bias(1, 1),
                        batch, seq)
    xf = ffn(xf, ffn2_w_gate_up, ffn2_w_down, 1, 2)
    return xf.reshape(batch, seq, D_MODEL)
```

```python
import functools

import jax
import jax.numpy as jnp
from jax import lax
from jax.experimental import pallas as pl
from jax.experimental.pallas import tpu as pltpu

D_MODEL = 1024
DEPTH = 2
HEAD_DIM = 64
N_MIX_HEADS = 12
N_MEM_HEADS = 4
DILATED_GROUPS = ((128, 1), (512, 4), (2048, 16))
HEADS_PER_GROUP = N_MIX_HEADS // len(DILATED_GROUPS)
ROT_DIM = HEAD_DIM // 4
ROT_HALF = ROT_DIM // 2
ROPE_THETA = 500000.0
D_FF = 2816
BAND = 128
DEEPNORM_ALPHA = (2 * DEPTH) ** 0.25
LN_EPS = 1e-5
MIX_W = N_MIX_HEADS * HEAD_DIM
MEM_W = N_MEM_HEADS * HEAD_DIM
GROUP_W = HEADS_PER_GROUP * HEAD_DIM
ATTN_SCALE = HEAD_DIM ** -0.5

LANES = 128
VMEM_LIMIT = 56 * 1024 * 1024

ROW_TILE = 1024
FFN_TILE = 512
FF_CHUNK = 256
FOX_TILE = 512
FOX_AUG = 16
FOX_SLOTS = 2
DIL_TILE = 1024

F32 = jnp.float32
BF16 = jnp.bfloat16
NEG = -1e30
LOG2_E = 1.4426950408889634


def _cparams(*sem):
    return pltpu.CompilerParams(dimension_semantics=sem, vmem_limit_bytes=VMEM_LIMIT)


def _resident(shape):
    return pl.BlockSpec(shape, lambda *_: (0,) * len(shape), pipeline_mode=pl.Buffered(1))


def _dot(a, b):
    return jnp.dot(a, b, preferred_element_type=F32)


def _dot_nt(a, b):
    return lax.dot_general(a, b, (((1,), (1,)), ((), ())), preferred_element_type=F32)


def _layer_norm(z, g, b):
    mu = jnp.mean(z, axis=-1, keepdims=True)
    zc = z - mu
    var = jnp.mean(zc * zc, axis=-1, keepdims=True)
    return zc * lax.rsqrt(var + LN_EPS) * g + b


def _ffn_kernel(x_ref, wgu_ref, wd_ref, g_ref, b_ref, o_ref, h_ref):
    x = x_ref[...]
    xb = x.astype(BF16)
    for c in range(D_FF // FF_CHUNK):
        lo = c * FF_CHUNK
        gate = _dot(xb, wgu_ref[0, :, lo:lo + FF_CHUNK].astype(BF16))
        up = _dot(xb, wgu_ref[0, :, D_FF + lo:D_FF + lo + FF_CHUNK].astype(BF16))
        h_ref[:, lo:lo + FF_CHUNK] = (gate * jax.nn.sigmoid(gate) * up).astype(BF16)
    y = _dot(h_ref[...], wd_ref[0].astype(BF16))
    o_ref[...] = _layer_norm(DEEPNORM_ALPHA * x + 0.5 * y, g_ref[...], b_ref[...])


def _ffn(x, wgu_all, wd_all, layer, g, b):
    n = x.shape[0]
    row = pl.BlockSpec((FFN_TILE, D_MODEL), lambda i: (i, 0))
    one_layer = lambda shape: pl.BlockSpec((1,) + shape, lambda i: (layer, 0, 0), pipeline_mode=pl.Buffered(1))
    return pl.pallas_call(
        _ffn_kernel,
        grid=(n // FFN_TILE,),
        in_specs=[row, one_layer((D_MODEL, 2 * D_FF)), one_layer((D_FF, D_MODEL)),
                  _resident((1, D_MODEL)), _resident((1, D_MODEL))],
        out_specs=row,
        out_shape=jax.ShapeDtypeStruct((n, D_MODEL), F32),
        scratch_shapes=[pltpu.VMEM((FFN_TILE, D_FF), BF16)],
        compiler_params=_cparams("parallel"),
        name="ffn_ln",
    )(x, wgu_all, wd_all, g, b)


def _memkv_kernel(m_ref, w_ref, o_ref):
    o_ref[...] = _dot(m_ref[...].astype(BF16), w_ref[...].astype(BF16)).astype(BF16)


def _memkv(mem2d, w):
    n = mem2d.shape[0]
    tile = min(ROW_TILE, n)
    return pl.pallas_call(
        _memkv_kernel,
        grid=(n // tile,),
        in_specs=[pl.BlockSpec((tile, D_MODEL), lambda i: (i, 0)), _resident((D_MODEL, 2 * MEM_W))],
        out_specs=pl.BlockSpec((tile, 2 * MEM_W), lambda i: (i, 0)),
        out_shape=jax.ShapeDtypeStruct((n, 2 * MEM_W), BF16),
        compiler_params=_cparams("parallel"),
        name="mem_kv",
    )(mem2d, w)


def _head_of_pair(block, odd):
    lane = lax.broadcasted_iota(jnp.int32, block.shape, 1)
    keep = lane >= HEAD_DIM if odd else lane < HEAD_DIM
    return jnp.where(keep, block, jnp.zeros_like(block))


def _pair_softmax_values(s_even, s_odd, v_pair):
    ones = jnp.ones((v_pair.shape[0], LANES), BF16)
    res, stats = [], []
    for s in (s_even, s_odd):
        m = jnp.max(s, axis=-1, keepdims=True)
        p = jnp.exp(s - m).astype(BF16)
        l = _dot(p, ones)
        res.append(_dot(p, v_pair) / l)
        stats.append((m, l))
    lane = lax.broadcasted_iota(jnp.int32, res[0].shape, 1)
    return jnp.where(lane < HEAD_DIM, res[0], res[1]), stats


def _memory_scores(qm, kv_ref):
    scores = []
    for pair in range(MEM_W // LANES):
        lanes = slice(pair * LANES, (pair + 1) * LANES)
        km = kv_ref[0, :, lanes]
        scores += [_dot_nt(_head_of_pair(qm[:, lanes], odd), km) for odd in (False, True)]
    return scores


def _memory_values(scores, kv_ref):
    outs = []
    for pair in range(MEM_W // LANES):
        vm = kv_ref[0, :, MEM_W + pair * LANES:MEM_W + (pair + 1) * LANES]
        outs.append(_pair_softmax_values(scores[2 * pair], scores[2 * pair + 1], vm)[0])
    return jnp.concatenate(outs, axis=-1)


def _proj_a_kernel(x_ref, w_ref, ca_ref, cm_ref, cp_ref, o0_ref, o1_ref, o2_ref, qm_ref, stage_ref):
    xb = x_ref[...].astype(BF16)
    ca, cm, cp = ca_ref[...], cm_ref[...], cp_ref[...]
    blocks = GROUP_W // LANES
    for g, o_ref in enumerate((o0_ref, o1_ref, o2_ref)):
        dilation = DILATED_GROUPS[g][1]
        per = ROW_TILE // dilation
        for part in range(3):
            lo = part * MIX_W + g * GROUP_W
            h = _dot(xb, w_ref[:, lo:lo + GROUP_W].astype(BF16))
            for cb in range(blocks):
                t = h[:, cb * LANES:(cb + 1) * LANES]
                if part < 2:
                    t = t * ca + pltpu.roll(t, LANES - ROT_HALF, 1) * cm + pltpu.roll(t, ROT_HALF, 1) * cp
                if part == 0:
                    t = t * ATTN_SCALE
                if dilation == 1:
                    out_lo = part * GROUP_W + cb * LANES
                    o_ref[0, 0, :, out_lo:out_lo + LANES] = t.astype(BF16)
                else:
                    stage_ref[part * blocks + cb] = t
        if dilation > 1:
            for j in range(dilation):
                rows = [stage_ref[cb, pl.ds(j, per, stride=dilation), :] for cb in range(3 * blocks)]
                o_ref[0, j] = jnp.concatenate(rows, axis=1).astype(BF16)
    qm_ref[...] = (_dot(xb, w_ref[:, 3 * MIX_W:].astype(BF16)) * ATTN_SCALE).astype(BF16)


def _proj_a(x, w, ca, cm, cp, batch, seq):
    per_seq = seq // ROW_TILE
    tab = pl.BlockSpec((ROW_TILE, LANES), lambda b, i: (i, 0))
    grouped = lambda d: pl.BlockSpec((1, d, ROW_TILE // d, 3 * GROUP_W), lambda b, i: (b, 0, i, 0))
    return pl.pallas_call(
        _proj_a_kernel,
        grid=(batch, per_seq),
        in_specs=[pl.BlockSpec((ROW_TILE, D_MODEL), lambda b, i: (b * per_seq + i, 0)),
                  _resident((D_MODEL, 3 * MIX_W + MEM_W)), tab, tab, tab],
        out_specs=[grouped(d) for _, d in DILATED_GROUPS]
                  + [pl.BlockSpec((ROW_TILE, MEM_W), lambda b, i: (b * per_seq + i, 0))],
        out_shape=[jax.ShapeDtypeStruct((batch, d, seq // d, 3 * GROUP_W), BF16) for _, d in DILATED_GROUPS]
                  + [jax.ShapeDtypeStruct((batch * seq, MEM_W), BF16)],
        scratch_shapes=[pltpu.VMEM((3 * GROUP_W // LANES, ROW_TILE, LANES), F32)],
        compiler_params=_cparams("parallel", "parallel"),
        name="proj_a",
    )(x, w, ca, cm, cp)


def _dilated_kernel(cur_ref, prev_ref, o_ref, lse_ref, *, dilation, tile):
    first_key = jnp.where(pl.program_id(1) == 0, BAND, 0)
    qi = lax.broadcasted_iota(jnp.int32, (BAND, 2 * BAND), 0)
    ki = lax.broadcasted_iota(jnp.int32, (BAND, 2 * BAND), 1)
    band = (ki >= qi) & (ki <= qi + BAND)
    blocks = tile // BAND
    pairs = GROUP_W // LANES
    lane = lax.broadcasted_iota(jnp.int32, (BAND, LANES), 1)
    classes = cur_ref.shape[1]
    for cls in range(classes):
        scores, values = [], []
        for sb in range(blocks):
            rows = slice(sb * BAND, (sb + 1) * BAND)
            before = prev_ref[0, cls] if sb == 0 else cur_ref[0, cls, (sb - 1) * BAND:sb * BAND, :]
            mask = band & (ki >= first_key) if sb == 0 else band
            kv = jnp.concatenate([before, cur_ref[0, cls, rows, :]], axis=0)
            for pair in range(pairs):
                lanes = slice(pair * LANES, (pair + 1) * LANES)
                q = cur_ref[0, cls, rows, lanes]
                k = kv[:, GROUP_W + pair * LANES:GROUP_W + (pair + 1) * LANES]
                scores.append([jnp.where(mask, _dot_nt(_head_of_pair(q, odd), k), NEG) for odd in (False, True)])
                values.append(kv[:, 2 * GROUP_W + pair * LANES:2 * GROUP_W + (pair + 1) * LANES])
        outs, lses = [], []
        for (s_even, s_odd), v in zip(scores, values):
            o, ((m0, l0), (m1, l1)) = _pair_softmax_values(s_even, s_odd, v)
            outs.append(o)
            lses.append(jnp.where(lane < HEAD_DIM, m0 + jnp.log(l0), m1 + jnp.log(l1)))
        j = pl.program_id(2) * classes + cls
        for c in range(pairs):
            o = jnp.concatenate([outs[sb * pairs + c] for sb in range(blocks)], axis=0)
            lse = jnp.concatenate([lses[sb * pairs + c] for sb in range(blocks)], axis=0)
            if dilation == 1:
                o_ref[c, 0] = o
                lse_ref[c, 0] = lse
            else:
                o_ref[c, 0, pl.ds(j, tile, stride=dilation), :] = o
                lse_ref[c, 0, pl.ds(j, tile, stride=dilation), :] = lse


def _dilated_group(qkv, batch, seq, dilation):
    length = seq // dilation
    tile = min(DIL_TILE, length)
    assert length % tile == 0 and tile % BAND == 0
    per = tile // BAND
    halves = GROUP_W // LANES
    classes = min(dilation, DIL_TILE // tile)
    out_block = pl.BlockSpec((halves, 1, dilation * tile, LANES), lambda b, t, j: (0, b, t, 0))
    return pl.pallas_call(
        functools.partial(_dilated_kernel, dilation=dilation, tile=tile),
        grid=(batch, length // tile, dilation // classes),
        in_specs=[pl.BlockSpec((1, classes, tile, 3 * GROUP_W), lambda b, t, j: (b, j, t, 0)),
                  pl.BlockSpec((1, classes, BAND, 3 * GROUP_W),
                               lambda b, t, j: (b, j, jnp.maximum(t * per - 1, 0), 0))],
        out_specs=[out_block, out_block],
        out_shape=[jax.ShapeDtypeStruct((halves, batch, seq, LANES), F32)] * 2,
        compiler_params=_cparams("parallel", "parallel", "arbitrary"),
        name=f"dilated_attn_d{dilation}",
    )(qkv, qkv)


def _out_a_kernel(x_ref, o0_ref, o1_ref, o2_ref, l0_ref, l1_ref, l2_ref, qm_ref, kv_ref, w_ref, g_ref, b_ref,
                  y_ref):
    scores = _memory_scores(qm_ref[...], kv_ref)
    halves = []
    for c in range(GROUP_W // LANES):
        l0, l1, l2 = l0_ref[c], l1_ref[c], l2_ref[c]
        m = jnp.maximum(jnp.maximum(l0, l1), l2)
        e0, e1, e2 = jnp.exp(l0 - m), jnp.exp(l1 - m), jnp.exp(l2 - m)
        halves.append((e0 * o0_ref[c] + e1 * o1_ref[c] + e2 * o2_ref[c]) / (e0 + e1 + e2))
    o_a = jnp.concatenate(halves, axis=1)
    mix = _dot(o_a.astype(BF16), w_ref[:GROUP_W, :].astype(BF16))
    o_m = _memory_values(scores, kv_ref)
    mix = mix + _dot(o_m.astype(BF16), w_ref[GROUP_W:, :].astype(BF16))
    y_ref[...] = _layer_norm(DEEPNORM_ALPHA * x_ref[...] + mix, g_ref[...], b_ref[...])


def _out_a(x, os, lses, qm, kv, w, g, b, seq):
    n = x.shape[0]
    per_seq = seq // ROW_TILE
    row = lambda w_: pl.BlockSpec((ROW_TILE, w_), lambda i: (i, 0))
    return pl.pallas_call(
        _out_a_kernel,
        grid=(n // ROW_TILE,),
        in_specs=[row(D_MODEL)] + [pl.BlockSpec((GROUP_W // LANES, ROW_TILE, LANES), lambda i: (0, i, 0))] * 6
                 + [row(MEM_W),
                  pl.BlockSpec((1, kv.shape[1], 2 * MEM_W), lambda i: (i // per_seq, 0, 0)),
                  _resident((GROUP_W + MEM_W, D_MODEL)), _resident((1, D_MODEL)), _resident((1, D_MODEL))],
        out_specs=row(D_MODEL),
        out_shape=jax.ShapeDtypeStruct((n, D_MODEL), F32),
        compiler_params=_cparams("parallel"),
        name="out_a_ln",
    )(x, *os, *lses, qm, kv, w, g, b)


def _decay_terms(xb, wf_ref, fb_ref, carry_ref):
    @pl.when(pl.program_id(1) == 0)
    def _():
        carry_ref[...] = jnp.zeros_like(carry_ref)

    f = _dot(xb, wf_ref[...])
    ft = f.T[:FOX_AUG] + fb_ref[...]
    logf = jnp.minimum(ft, 0.0) - jnp.log1p(jnp.exp(-jnp.abs(ft)))
    lane = lax.broadcasted_iota(jnp.int32, logf.shape, 1)
    c = logf
    shift = 1
    while shift < ROW_TILE:
        c = c + jnp.where(lane >= shift, pltpu.roll(c, shift, 1), 0.0)
        shift *= 2
    c = c + carry_ref[...]
    carry_ref[...] = c[:, ROW_TILE - 1:ROW_TILE]
    c = c * LOG2_E
    hi = c.astype(BF16)
    r1 = c - hi.astype(F32)
    mid = r1.astype(BF16)
    lo = (r1 - mid.astype(F32)).astype(BF16)
    ones = jnp.ones((FOX_AUG, ROW_TILE), BF16)
    zeros = jnp.zeros((LANES - 4 * FOX_AUG, ROW_TILE), BF16)
    return jnp.concatenate([hi, mid, lo, ones, zeros], axis=0)


def _proj_b_kernel(x_ref, wf_ref, fb_ref, wqt_ref, wk_ref, wvt_ref, wqm_ref, pq_ref, pk_ref,
                   qt_ref, k_ref, vt_ref, qm_ref, carry_ref):
    xb = x_ref[...].astype(BF16)
    stack = _decay_terms(xb, wf_ref, fb_ref, carry_ref)
    qt = _dot_nt(wqt_ref[...], xb)
    vt = _dot_nt(wvt_ref[...], xb)
    kk = _dot(xb, wk_ref[...])
    aug = _dot(pq_ref[...], stack)
    bias = _dot(stack.astype(F32).T.astype(BF16), pk_ref[...])
    low_lanes = lax.broadcasted_iota(jnp.int32, (ROW_TILE, LANES), 1) < HEAD_DIM
    for pair in range(MIX_W // LANES):
        lanes = slice(pair * LANES, (pair + 1) * LANES)
        k_ref[0, 2 * pair] = jnp.where(low_lanes, kk[:, lanes], bias[:, lanes]).astype(BF16)
        k_ref[0, 2 * pair + 1] = jnp.where(low_lanes, bias[:, lanes], kk[:, lanes]).astype(BF16)
    row = lax.broadcasted_iota(jnp.int32, (FOX_AUG, ROW_TILE), 0)
    ones_row = jnp.where(row == 0, 1.0, 0.0).astype(BF16)
    for h in range(N_MIX_HEADS):
        hs = slice(h * HEAD_DIM, (h + 1) * HEAD_DIM)
        q_rows, aug_rows = (slice(0, HEAD_DIM), slice(HEAD_DIM, LANES)) if h % 2 == 0 else \
                           (slice(HEAD_DIM, LANES), slice(0, HEAD_DIM))
        qt_ref[0, h, q_rows, :] = qt[hs].astype(BF16)
        qt_ref[0, h, aug_rows, :] = aug[hs].astype(BF16)
        vt_ref[0, h, :HEAD_DIM, :] = vt[hs].astype(BF16)
        vt_ref[0, h, HEAD_DIM:, :] = ones_row
    qm_ref[...] = _dot(xb, wqm_ref[...]).astype(BF16)


def _proj_b(x, wf, fb, wqt, wk, wvt, wqm, pq, pk, batch, seq):
    per_seq = seq // ROW_TILE
    v_rows = HEAD_DIM + FOX_AUG
    pairs = MIX_W // LANES
    rows = lambda w_: pl.BlockSpec((ROW_TILE, w_), lambda b, i: (b * per_seq + i, 0))
    return pl.pallas_call(
        _proj_b_kernel,
        grid=(batch, per_seq),
        in_specs=[rows(D_MODEL), _resident(wf.shape), _resident(fb.shape),
                  _resident(wqt.shape), _resident(wk.shape), _resident(wvt.shape), _resident(wqm.shape),
                  _resident(pq.shape), _resident(pk.shape)],
        out_specs=[pl.BlockSpec((1, N_MIX_HEADS, LANES, ROW_TILE), lambda b, i: (b, 0, 0, i)),
                   pl.BlockSpec((1, N_MIX_HEADS, ROW_TILE, LANES), lambda b, i: (b, 0, i, 0)),
                   pl.BlockSpec((1, N_MIX_HEADS, v_rows, ROW_TILE), lambda b, i: (b, 0, 0, i)),
                   rows(MEM_W)],
        out_shape=[jax.ShapeDtypeStruct((batch, N_MIX_HEADS, LANES, seq), BF16),
                   jax.ShapeDtypeStruct((batch, N_MIX_HEADS, seq, LANES), BF16),
                   jax.ShapeDtypeStruct((batch, N_MIX_HEADS, v_rows, seq), BF16),
                   jax.ShapeDtypeStruct((batch * seq, MEM_W), BF16)],
        scratch_shapes=[pltpu.VMEM((FOX_AUG, 1), F32)],
        compiler_params=_cparams("parallel", "arbitrary"),
        name="proj_b",
    )(x, wf, fb, wqt, wk, wvt, wqm, pq, pk)


def _fox_kernel(qt_ref, k_ref, vt_ref, o_ref, s_ref, p_ref):
    seq = k_ref.shape[2]
    heads = k_ref.shape[1]
    key_pos = lax.broadcasted_iota(jnp.int32, (FOX_TILE, FOX_TILE), 0)
    qry_pos = lax.broadcasted_iota(jnp.int32, (FOX_TILE, FOX_TILE), 1)
    future = key_pos > qry_pos
    items = [(qi, h) for qi in range(seq // FOX_TILE) for h in range(heads)]

    def scores(i):
        qi, h = items[i]
        q = qt_ref[0, h, :, qi * FOX_TILE:(qi + 1) * FOX_TILE]
        m = None
        for kc in range(qi + 1):
            rows = slice(kc * FOX_TILE, (kc + 1) * FOX_TILE)
            s = _dot(k_ref[0, h, rows, :], q)
            if kc == qi:
                s = jnp.where(future, NEG, s)
            s_ref[i % FOX_SLOTS, rows, :] = s
            mc = jnp.max(s, axis=0, keepdims=True)
            m = mc if m is None else jnp.maximum(m, mc)
        return m

    ahead = FOX_SLOTS - 1
    maxima = [scores(i) for i in range(min(ahead, len(items)))]
    outs = []
    for i, (qi, h) in enumerate(items):
        if i + ahead < len(items):
            maxima.append(scores(i + ahead))
        m = maxima[i]
        n_keys = (qi + 1) * FOX_TILE
        for kc in range(qi + 1):
            rows = slice(kc * FOX_TILE, (kc + 1) * FOX_TILE)
            p_ref[i % 2, rows, :] = jnp.exp2(s_ref[i % FOX_SLOTS, rows, :] - m).astype(BF16)
        acc = _dot(vt_ref[0, h, :, :n_keys], p_ref[i % 2, :n_keys, :])
        outs.append(acc[:HEAD_DIM] / acc[HEAD_DIM:HEAD_DIM + 1])
        if h == heads - 1:
            o_ref[0, qi * FOX_TILE:(qi + 1) * FOX_TILE, :] = jnp.concatenate(outs, axis=0).T.astype(BF16)
            outs = []


def _fox(qt, kk, vt, batch, seq):
    pair = LANES // HEAD_DIM
    v_rows = HEAD_DIM + FOX_AUG
    return pl.pallas_call(
        _fox_kernel,
        grid=(batch, N_MIX_HEADS // pair),
        in_specs=[pl.BlockSpec((1, pair, LANES, seq), lambda b, h: (b, h, 0, 0)),
                  pl.BlockSpec((1, pair, seq, LANES), lambda b, h: (b, h, 0, 0)),
                  pl.BlockSpec((1, pair, v_rows, seq), lambda b, h: (b, h, 0, 0))],
        out_specs=pl.BlockSpec((1, seq, LANES), lambda b, h: (b, 0, h)),
        out_shape=jax.ShapeDtypeStruct((batch, seq, MIX_W), BF16),
        scratch_shapes=[pltpu.VMEM((FOX_SLOTS, seq, FOX_TILE), F32), pltpu.VMEM((2, seq, FOX_TILE), BF16)],
        compiler_params=_cparams("parallel", "parallel"),
        name="fox_attn",
    )(qt, kk, vt)


def _out_b_kernel(x_ref, o_ref, qm_ref, kv_ref, w_ref, g_ref, b_ref, y_ref):
    scores = _memory_scores(qm_ref[...], kv_ref)
    mix = _dot(o_ref[...], w_ref[:MIX_W, :].astype(BF16))
    o_m = _memory_values(scores, kv_ref)
    mix = mix + _dot(o_m.astype(BF16), w_ref[MIX_W:, :].astype(BF16))
    y_ref[...] = _layer_norm(DEEPNORM_ALPHA * x_ref[...] + mix, g_ref[...], b_ref[...])


def _out_b(x, o, qm, kv, w, g, b, seq):
    n = x.shape[0]
    per_seq = seq // ROW_TILE
    row = lambda w_: pl.BlockSpec((ROW_TILE, w_), lambda i: (i, 0))
    return pl.pallas_call(
        _out_b_kernel,
        grid=(n // ROW_TILE,),
        in_specs=[row(D_MODEL), row(MIX_W), row(MEM_W),
                  pl.BlockSpec((1, kv.shape[1], 2 * MEM_W), lambda i: (i // per_seq, 0, 0)),
                  _resident((MIX_W + MEM_W, D_MODEL)), _resident((1, D_MODEL)), _resident((1, D_MODEL))],
        out_specs=row(D_MODEL),
        out_shape=jax.ShapeDtypeStruct((n, D_MODEL), F32),
        compiler_params=_cparams("parallel"),
        name="out_b_ln",
    )(x, o, qm, kv, w, g, b)


def _rope_tables(seq):
    pos = jnp.arange(seq, dtype=F32)
    inv_freq = 1.0 / (ROPE_THETA ** (jnp.arange(ROT_HALF, dtype=F32) / ROT_HALF))
    ang = pos[:, None] * inv_freq[None, :]
    cos, sin = jnp.cos(ang), jnp.sin(ang)
    ones = jnp.ones((seq, HEAD_DIM - ROT_DIM), F32)
    zeros = jnp.zeros((seq, HEAD_DIM - ROT_DIM), F32)
    zero_half = jnp.zeros((seq, ROT_HALF), F32)
    head = lambda parts: jnp.concatenate(parts * (LANES // HEAD_DIM), axis=1)
    return head([cos, cos, ones]), head([-sin, zero_half, zeros]), head([zero_half, sin, zeros])


def _placement_matrices():
    h = jnp.arange(N_MIX_HEADS)
    pq = jnp.zeros((MIX_W, LANES), F32)
    pk = jnp.zeros((LANES, MIX_W), F32)
    base = (h // 2) * LANES + jnp.where(h % 2 == 0, HEAD_DIM, 0)
    for part in range(3):
        pq = pq.at[h * HEAD_DIM + part, part * FOX_AUG + h].set(1.0)
        pq = pq.at[h * HEAD_DIM + 3 + part, 3 * FOX_AUG].set(1.0)
        pk = pk.at[3 * FOX_AUG, base + part].set(1.0)
        pk = pk.at[part * FOX_AUG + h, base + 3 + part].set(-1.0)
    return pq.astype(BF16), pk.astype(BF16)


def _layer_a_mixer(xf, mem, w_in, w_mem_kv, w_out, g, b, batch, seq):
    n = batch * seq
    ca, cm, cp = _rope_tables(seq)
    qkv0, qkv1, qkv2, qm = _proj_a(xf, w_in, ca, cm, cp, batch, seq)
    os, lses = [], []
    for qkv, (_, dilation) in zip((qkv0, qkv1, qkv2), DILATED_GROUPS):
        o, lse = _dilated_group(qkv, batch, seq, dilation)
        os.append(o.reshape(GROUP_W // LANES, n, LANES))
        lses.append(lse.reshape(GROUP_W // LANES, n, LANES))
    kv = _memkv(mem.reshape(batch * mem.shape[1], D_MODEL), w_mem_kv)
    kv = kv.reshape(batch, mem.shape[1], 2 * MEM_W)
    return _out_a(xf, os, lses, qm, kv, w_out, g, b, seq)


def _layer_b_mixer(xf, mem, w_in, forget_bias, w_mem_kv, w_out, g, b, batch, seq):
    n = batch * seq
    wf = jnp.pad(w_in[:, 3 * MIX_W:3 * MIX_W + N_MIX_HEADS], ((0, 0), (0, LANES - N_MIX_HEADS))).astype(BF16)
    fb = jnp.pad(forget_bias.astype(F32), (0, FOX_AUG - N_MIX_HEADS)).reshape(FOX_AUG, 1)
    pq, pk = _placement_matrices()
    wqt = (w_in[:, :MIX_W] * (ATTN_SCALE * LOG2_E)).T.astype(BF16)
    wk = w_in[:, MIX_W:2 * MIX_W].astype(BF16)
    wvt = w_in[:, 2 * MIX_W:3 * MIX_W].T.astype(BF16)
    wqm = (w_in[:, 3 * MIX_W + N_MIX_HEADS:] * ATTN_SCALE).astype(BF16)
    qt, kk, vt, qm = _proj_b(xf, wf, fb, wqt, wk, wvt, wqm, pq, pk, batch, seq)
    o = _fox(qt, kk, vt, batch, seq).reshape(n, MIX_W)
    kv = _memkv(mem.reshape(batch * mem.shape[1], D_MODEL), w_mem_kv)
    kv = kv.reshape(batch, mem.shape[1], 2 * MEM_W)
    return _out_b(xf, o, qm, kv, w_out, g, b, seq)


def kernel(x, mem, ffn1_w_gate_up, ffn1_w_down, ffn2_w_gate_up, ffn2_w_down, ln_gain, ln_bias, mem_w_kv,
           a_w_in, a_w_out, b_w_in, b_forget_bias, b_w_out):
    batch, seq, _ = x.shape
    n = batch * seq
    assert seq % ROW_TILE == 0 and all(seq % (d * BAND) == 0 and ROW_TILE % d == 0 for _, d in DILATED_GROUPS)
    xf = x.reshape(n, D_MODEL)
    gain = lambda i, j: ln_gain[i, j].reshape(1, D_MODEL)
    bias = lambda i, j: ln_bias[i, j].reshape(1, D_MODEL)
    ffn = lambda v, wgu, wd, i, j: _ffn(v, wgu, wd, i, gain(i, j), bias(i, j))

    xf = ffn(xf, ffn1_w_gate_up, ffn1_w_down, 0, 0)
    xf = _layer_a_mixer(xf, mem, a_w_in[0], mem_w_kv[0], a_w_out[0], gain(0, 1), bias(0, 1), batch, seq)
    xf = ffn(xf, ffn2_w_gate_up, ffn2_w_down, 0, 2)

    xf = ffn(xf, ffn1_w_gate_up, ffn1_w_down, 1, 0)
    xf = _layer_b_mixer(xf, mem, b_w_in[0], b_forget_bias[0], mem_w_kv[1], b_w_out[0], gain(1, 1), bias(1, 1),
                        batch, seq)
    xf = ffn(xf, ffn2_w_gate_up, ffn2_w_down, 1, 2)
    return xf.reshape(batch, seq, D_MODEL)
```
